```python
import math
import jax, jax.numpy as jnp
from jax import lax
import numpy as np

D_MODEL = 2048
BATCH = 2
SEQ = 16384
DEPTH = 2

GRID_W = 64
CTX_LEN = 256
HEAD_DIM = 128
D_MIX = D_MODEL
D_NA = D_MIX // 4
NA_HEADS = D_NA // HEAD_DIM
NA_WIN_H = 8
NA_WIN_W = 16
POOL_WINDOWS = (2, 4, 8, 16)
N_POOL = len(POOL_WINDOWS)
D_POOL = D_MIX // 4
POOL_GROUP = D_POOL // N_POOL
D_DA = D_MIX // 2
DA_V_DIM = 2 * HEAD_DIM
DA_QK_DIM = HEAD_DIM
DA_HEADS = D_DA // DA_V_DIM
D_DA_QK = 2 * DA_HEADS * DA_QK_DIM
D_IN = 3 * D_NA + D_POOL + 2 * D_DA_QK + D_DA
SPLIT_POINTS = (D_NA, 2 * D_NA, 3 * D_NA, 3 * D_NA + D_POOL,
                3 * D_NA + D_POOL + D_DA_QK, 3 * D_NA + D_POOL + 2 * D_DA_QK)
D_FF = 5632
CONV_W = 3
ROPE_BASE = 10000.0
EPS = 1e-6
Q_BLOCK = 128

kernel_name = "hybrid_natten_pool_diffattn_dit_block"


def rms_norm(x, g):
    xf = x.astype(jnp.float32)
    y = xf * lax.rsqrt(jnp.mean(xf * xf, axis=-1, keepdims=True) + EPS)
    return (y * g.astype(jnp.float32)).astype(x.dtype)


def ada_params(cond, w_mod, b_mod):
    m = jax.nn.silu(cond) @ w_mod + b_mod
    return jnp.split(m[..., None, :], 6, axis=-1)


def to_heads(t, n_heads):
    b, l, _ = t.shape
    return t.reshape(b, l, n_heads, -1).transpose(0, 2, 1, 3)


def merge_heads(t):
    b, h, l, d = t.shape
    return t.transpose(0, 2, 1, 3).reshape(b, l, h * d)


def split_projection(u):
    qa, ka, va, ub, qd, kd, vd = jnp.split(u, SPLIT_POINTS, axis=-1)
    qd = to_heads(qd, 2 * DA_HEADS)
    kd = to_heads(kd, 2 * DA_HEADS)
    return (to_heads(qa, NA_HEADS), to_heads(ka, NA_HEADS), to_heads(va, NA_HEADS), ub,
            qd[:, 0::2], qd[:, 1::2], kd[:, 0::2], kd[:, 1::2], to_heads(vd, DA_HEADS))


def axial_rope_tables(n_tokens, dtype):
    t = jnp.arange(n_tokens)
    n_freq = DA_QK_DIM // 4
    inv = ROPE_BASE ** (-jnp.arange(n_freq, dtype=jnp.float32) / n_freq)
    ang_r = (t // GRID_W).astype(jnp.float32)[:, None] * inv
    ang_c = (t % GRID_W).astype(jnp.float32)[:, None] * inv
    return (jnp.cos(ang_r).astype(dtype), jnp.sin(ang_r).astype(dtype),
            jnp.cos(ang_c).astype(dtype), jnp.sin(ang_c).astype(dtype))


def rotate_half(x, cos, sin):
    x1, x2 = jnp.split(x, 2, axis=-1)
    return jnp.concatenate([x1 * cos - x2 * sin, x1 * sin + x2 * cos], axis=-1)


def apply_axial_rope(x, tabs):
    cos_r, sin_r, cos_c, sin_c = tabs
    xr, xc = jnp.split(x, 2, axis=-1)
    return jnp.concatenate([rotate_half(xr, cos_r, sin_r), rotate_half(xc, cos_c, sin_c)], axis=-1)


def softmax_attention(q, k, v):
    s = jnp.einsum('bhqd,bhkd->bhqk', q, k).astype(jnp.float32) * (q.shape[-1] ** -0.5)
    p = jax.nn.softmax(s, axis=-1).astype(v.dtype)
    return jnp.einsum('bhqk,bhkd->bhqd', p, v)


def neighbourhood_attention(q, k, v, k_ctx, v_ctx, rel_bias):
    b, h, s, d = q.shape
    rows = s // GRID_W
    kh = min(NA_WIN_H, rows)
    kw = NA_WIN_W
    scale = d ** -0.5
    qg = q.reshape(b, h, rows, GRID_W, d)
    kg = k.reshape(b, h, rows, GRID_W, d)
    vg = v.reshape(b, h, rows, GRID_W, d)
    cols = jnp.arange(GRID_W)
    col_start = jnp.clip(cols - kw // 2, 0, GRID_W - kw)
    col_idx = col_start[:, None] + jnp.arange(kw)[None, :]
    col_off = col_idx - cols[:, None] + (NA_WIN_W - 1)

    def one_row(r):
        rs = jnp.clip(r - kh // 2, 0, rows - kh)
        k_nb = lax.dynamic_slice_in_dim(kg, rs, kh, axis=2)[:, :, :, col_idx, :]
        v_nb = lax.dynamic_slice_in_dim(vg, rs, kh, axis=2)[:, :, :, col_idx, :]
        q_r = lax.dynamic_index_in_dim(qg, r, axis=2, keepdims=False)
        row_off = rs + jnp.arange(kh) - r + (NA_WIN_H - 1)
        bias = rel_bias[:, row_off[:, None, None], col_off[None, :, :]].transpose(0, 2, 1, 3)
        s_lat = (jnp.einsum('bhqd,bhiqjd->bhqij', q_r, k_nb).astype(jnp.float32) * scale
                 + bias.astype(jnp.float32))
        s_ctx = jnp.einsum('bhqd,bhcd->bhqc', q_r, k_ctx).astype(jnp.float32) * scale
        p = jax.nn.softmax(jnp.concatenate([s_lat.reshape(b, h, GRID_W, kh * kw), s_ctx], axis=-1),
                           axis=-1).astype(v.dtype)
        p_lat = p[..., :kh * kw].reshape(b, h, GRID_W, kh, kw)
        p_ctx = p[..., kh * kw:]
        return (jnp.einsum('bhqij,bhiqjd->bhqd', p_lat, v_nb)
                + jnp.einsum('bhqc,bhcd->bhqd', p_ctx, v_ctx))

    out = lax.map(one_row, jnp.arange(rows))
    return out.transpose(1, 2, 0, 3, 4).reshape(b, h, s, d)


def multiscale_pool(u, w_pool, pool_scale):
    b, l, _ = u.shape
    uf = u.astype(jnp.float32)
    csum = jnp.pad(jnp.cumsum(uf, axis=1), ((0, 0), (1, 0), (0, 0)))
    t = jnp.arange(l)
    means = []
    for g, w in enumerate(POOL_WINDOWS):
        lo = jnp.clip(t - w // 2, 0, l)
        hi = jnp.clip(t + w // 2, 0, l)
        cg = csum[..., g * POOL_GROUP:(g + 1) * POOL_GROUP]
        means.append((cg[:, hi] - cg[:, lo]) / (hi - lo).astype(jnp.float32)[None, :, None])
    pooled = (jnp.concatenate(means, axis=-1) - uf).astype(u.dtype).reshape(b, l, N_POOL, POOL_GROUP)
    y = jnp.einsum('blgc,gcd->blgd', pooled, w_pool).reshape(b, l, D_POOL)
    return y * pool_scale


def diff_lambda(lq1, lk1, lq2, lk2, lam_init):
    f = lambda a: a.astype(jnp.float32)
    return jnp.exp(jnp.sum(f(lq1) * f(lk1))) - jnp.exp(jnp.sum(f(lq2) * f(lk2))) + lam_init


def differential_attention(q1, q2, k1, k2, v, lam):
    scale = q1.shape[-1] ** -0.5
    s1 = jnp.einsum('bhqd,bhkd->bhqk', q1, k1).astype(jnp.float32) * scale
    s2 = jnp.einsum('bhqd,bhkd->bhqk', q2, k2).astype(jnp.float32) * scale
    p = jax.nn.softmax(s1, axis=-1) - lam * jax.nn.softmax(s2, axis=-1)
    return jnp.einsum('bhqk,bhkd->bhqd', p.astype(v.dtype), v)


def blocked_differential_attention(q1, q2, k1, k2, v, lam):
    b, h, s, d = q1.shape
    nb = s // Q_BLOCK
    blocks = lambda q: q.reshape(b, h, nb, Q_BLOCK, d).transpose(2, 0, 1, 3, 4)
    out = lax.map(lambda qq: differential_attention(qq[0], qq[1], k1, k2, v, lam),
                  (blocks(q1), blocks(q2)))
    return out.transpose(1, 2, 0, 3, 4).reshape(b, h, s, v.shape[-1])


def conv_gated_mlp(h, w_up, conv_w, conv_b, w_down):
    a = h @ w_up
    ap = jnp.pad(a, ((0, 0), (1, 1), (0, 0)))
    a = ap[:, :-2] * conv_w[0] + ap[:, 1:-1] * conv_w[1] + ap[:, 2:] * conv_w[2] + conv_b
    gate, up = jnp.split(a, 2, axis=-1)
    return (jax.nn.silu(gate) * up) @ w_down


def setup_inputs(seed: int = 0) -> dict:
    key = jax.random.key(seed)
    ks = jax.random.split(key, 26)
    f32 = jnp.float32
    nrm = lambda k, shape, sc: jax.random.normal(k, shape, f32) * sc
    D, F, L = D_MODEL, D_FF, DEPTH
    return {
        "x": nrm(ks[0], (BATCH, SEQ, D), 1.0),
        "c": nrm(ks[1], (BATCH, D), 1.0),
        "ctx": nrm(ks[2], (BATCH, CTX_LEN, D), 1.0),
        "c_ctx": nrm(ks[3], (D,), 1.0),
        "w_mod": nrm(ks[4], (L, D, 6 * D), 0.5 * D ** -0.5),
        "b_mod": nrm(ks[5], (L, 6 * D), 0.01),
        "g_mix": 1.0 + nrm(ks[6], (L, D), 0.05),
        "w_in": nrm(ks[7], (L, D, D_IN), D ** -0.5),
        "na_bias": nrm(ks[8], (L, NA_HEADS, 2 * NA_WIN_H - 1, 2 * NA_WIN_W - 1), 0.1),
        "w_pool": nrm(ks[9], (L, N_POOL, POOL_GROUP, POOL_GROUP), POOL_GROUP ** -0.5),
        "pool_scale": 1.0 + nrm(ks[10], (L, D_POOL), 0.1),
        "lambda_q1": nrm(ks[11], (L, DA_QK_DIM), 0.1),
        "lambda_k1": nrm(ks[12], (L, DA_QK_DIM), 0.1),
        "lambda_q2": nrm(ks[13], (L, DA_QK_DIM), 0.1),
        "lambda_k2": nrm(ks[14], (L, DA_QK_DIM), 0.1),
        "g_subln": 1.0 + nrm(ks[15], (L, DA_V_DIM), 0.05),
        "w_out": nrm(ks[16], (L, D_MIX, D), D_MIX ** -0.5),
        "g_ffn": 1.0 + nrm(ks[17], (L, D), 0.05),
        "w_up": nrm(ks[18], (L, D, 2 * F), D ** -0.5),
        "conv_w": nrm(ks[19], (L, CONV_W, 2 * F), CONV_W ** -0.5),
        "conv_b": nrm(ks[20], (L, 2 * F), 0.01),
        "w_down": nrm(ks[21], (L, F, D), F ** -0.5),
        "g_final": 1.0 + nrm(ks[22], (D,), 0.05),
    }


def reference(x, c, ctx, c_ctx, w_mod, b_mod, g_mix, w_in, na_bias, w_pool, pool_scale,
              lambda_q1, lambda_k1, lambda_q2, lambda_k2, g_subln, w_out, g_ffn, w_up,
              conv_w, conv_b, w_down, g_final):
    n_tok = x.shape[1]
    rope = axial_rope_tables(n_tok, x.dtype)
    for layer in range(DEPTH):
        last = layer == DEPTH - 1
        lam_init = 0.8 - 0.6 * math.exp(-0.3 * layer)
        sh_x, sc_x, ga_x, shf_x, scf_x, gf_x = ada_params(c, w_mod[layer], b_mod[layer])
        sh_c, sc_c, ga_c, shf_c, scf_c, gf_c = ada_params(c_ctx, w_mod[layer], b_mod[layer])

        hx = rms_norm(x, g_mix[layer]) * (1 + sc_x) + sh_x
        hc = rms_norm(ctx, g_mix[layer]) * (1 + sc_c) + sh_c
        qa_x, ka_x, va_x, ub_x, q1_x, q2_x, k1_x, k2_x, vd_x = split_projection(hx @ w_in[layer])
        qa_c, ka_c, va_c, ub_c, q1_c, q2_c, k1_c, k2_c, vd_c = split_projection(hc @ w_in[layer])

        oa_x = neighbourhood_attention(qa_x, ka_x, va_x, ka_c, va_c, na_bias[layer])
        ob_x = multiscale_pool(ub_x, w_pool[layer], pool_scale[layer])
        lam = diff_lambda(lambda_q1[layer], lambda_k1[layer], lambda_q2[layer], lambda_k2[layer], lam_init)
        k1_all = jnp.concatenate([k1_c, apply_axial_rope(k1_x, rope)], axis=2)
        k2_all = jnp.concatenate([k2_c, apply_axial_rope(k2_x, rope)], axis=2)
        v_all = jnp.concatenate([vd_c, vd_x], axis=2)
        oc_x = blocked_differential_attention(apply_axial_rope(q1_x, rope), apply_axial_rope(q2_x, rope),
                                              k1_all, k2_all, v_all, lam)
        oc_x = rms_norm(oc_x, g_subln[layer]) * (1.0 - lam_init)

        y_x = jnp.concatenate([merge_heads(oa_x), ob_x, merge_heads(oc_x)], axis=-1) @ w_out[layer]
        x_new = x + ga_x * y_x
        hx2 = rms_norm(x_new, g_ffn[layer]) * (1 + scf_x) + shf_x
        x_new = x_new + gf_x * conv_gated_mlp(hx2, w_up[layer], conv_w[layer], conv_b[layer], w_down[layer])

        if not last:
            oa_c = softmax_attention(qa_c, ka_c, va_c)
            ob_c = multiscale_pool(ub_c, w_pool[layer], pool_scale[layer])
            oc_c = differential_attention(q1_c, q2_c, k1_c, k2_c, vd_c, lam)
            oc_c = rms_norm(oc_c, g_subln[layer]) * (1.0 - lam_init)
            y_c = jnp.concatenate([merge_heads(oa_c), ob_c, merge_heads(oc_c)], axis=-1) @ w_out[layer]
            ctx_new = ctx + ga_c * y_c
            hc2 = rms_norm(ctx_new, g_ffn[layer]) * (1 + scf_c) + shf_c
            ctx = ctx_new + gf_c * conv_gated_mlp(hc2, w_up[layer], conv_w[layer], conv_b[layer], w_down[layer])
        x = x_new
    return rms_norm(x, g_final)
```

```python
import functools
import math

import jax
import jax.numpy as jnp
from jax import lax
from jax.experimental import pallas as pl
from jax.experimental.pallas import tpu as pltpu

F32 = jnp.float32
BF16 = jnp.bfloat16

GRID_W = 64
HEAD_DIM = 128
NA_HEADS = 4
NA_WIN_H = 8
NA_WIN_W = 16
NA_ROWS_PER_BLOCK = 4
POOL_WINDOWS = (2, 4, 8, 16)
POOL_GROUP = 128
POOL_HALO = 8
DA_HEADS = 4
DA_V_DIM = 256
CONV_HALO = 16
ROPE_BASE = 10000.0
EPS = 1e-6
NEG = -1e30
VMEM_LIMIT = 56 * 1024 * 1024


def _cparams(sem):
    return pltpu.CompilerParams(dimension_semantics=sem, vmem_limit_bytes=VMEM_LIMIT)


def _dot(a, b):
    return jnp.dot(a, b, preferred_element_type=F32)


def _dot_nt(a, b):
    return lax.dot_general(a, b, (((1,), (1,)), ((), ())), preferred_element_type=F32)


def _mod_kernel(ct_ref, w_ref, b_ref, o_ref, *, n_cond):
    ct = ct_ref[...]
    a = ct * jax.nn.sigmoid(ct)
    w = w_ref[...]
    rows = [jnp.sum(a[:, r:r + 1] * w, axis=0, keepdims=True) for r in range(n_cond)]
    rows.append(jnp.zeros((8 - n_cond, w.shape[1]), F32))
    o_ref[...] = jnp.concatenate(rows, axis=0) + b_ref[...]


def ada_modulation(cond_t, w_mod, b_mod, n_cond):
    depth, d, n = w_mod.shape
    tn = 1024
    return pl.pallas_call(
        functools.partial(_mod_kernel, n_cond=n_cond),
        out_shape=jax.ShapeDtypeStruct((depth, 8, n), F32),
        grid=(depth, n // tn),
        in_specs=[
            pl.BlockSpec((d, 8), lambda l, j: (0, 0)),
            pl.BlockSpec((None, d, tn), lambda l, j: (l, 0, j)),
            pl.BlockSpec((None, 1, tn), lambda l, j: (l, 0, j)),
        ],
        out_specs=pl.BlockSpec((None, 8, tn), lambda l, j: (l, 0, j)),
        compiler_params=_cparams(("arbitrary", "arbitrary")),
        name="ada_modulation",
    )(cond_t, w_mod, b_mod.reshape(depth, 1, n))


def _swap32(x):
    lane = lax.broadcasted_iota(jnp.int32, x.shape, 1)
    return jnp.where((lane % 64) < 32, pltpu.roll(x, 96, axis=1), pltpu.roll(x, 32, axis=1))


def _inproj_kernel(x_ref, g_ref, sh_ref, sc_ref, w_ref, cos_ref, sin_ref, u_ref, ub_ref, h_scr,
                   *, pool_block, rope_blocks):
    j = pl.program_id(1)

    @pl.when(j == 0)
    def _():
        x = x_ref[...]
        y = x * lax.rsqrt(jnp.mean(x * x, axis=-1, keepdims=True) + EPS)
        h = y * (g_ref[...] * (1.0 + sc_ref[...])) + sh_ref[...]
        h_scr[...] = h.astype(BF16)

    acc = _dot(h_scr[...], w_ref[...])

    is_rope = (j >= rope_blocks[0]) & (j < rope_blocks[1])

    @pl.when(is_rope)
    def _():
        cos = cos_ref[...]
        sin = sin_ref[...]
        for c in range(acc.shape[1] // HEAD_DIM):
            xc = acc[:, c * HEAD_DIM:(c + 1) * HEAD_DIM]
            u_ref[:, c * HEAD_DIM:(c + 1) * HEAD_DIM] = (xc * cos + _swap32(xc) * sin).astype(BF16)

    @pl.when(jnp.logical_not(is_rope))
    def _():
        u_ref[...] = acc.astype(BF16)

    @pl.when(j == pool_block)
    def _():
        ub_ref[...] = acc


def norm_inproj(x, g, mod4, layer, mod_rows, w_in, cos, sin, *, tm, seq):
    r, d = x.shape
    d_in = w_in.shape[1]
    tn = 512
    tiles_per_seq = seq // tm
    pool_block = 1536 // tn
    rope_blocks = (2048 // tn, 4096 // tn)

    def mod_spec(chunk):
        return pl.BlockSpec((None, None, 1, d),
                            lambda i, j: (layer, mod_rows(i // tiles_per_seq), 0, chunk))

    return pl.pallas_call(
        functools.partial(_inproj_kernel, pool_block=pool_block, rope_blocks=rope_blocks),
        out_shape=(jax.ShapeDtypeStruct((r, d_in), BF16), jax.ShapeDtypeStruct((r, 512), F32)),
        grid=(r // tm, d_in // tn),
        in_specs=[
            pl.BlockSpec((tm, d), lambda i, j: (i, 0)),
            pl.BlockSpec((1, d), lambda i, j: (0, 0)),
            mod_spec(0),
            mod_spec(1),
            pl.BlockSpec((d, tn), lambda i, j: (0, j)),
            pl.BlockSpec((tm, HEAD_DIM), lambda i, j: (i % tiles_per_seq, 0)),
            pl.BlockSpec((tm, HEAD_DIM), lambda i, j: (i % tiles_per_seq, 0)),
        ],
        out_specs=(pl.BlockSpec((tm, tn), lambda i, j: (i, j)),
                   pl.BlockSpec((tm, 512), lambda i, j: (i, 0))),
        scratch_shapes=[pltpu.VMEM((tm, d), BF16)],
        compiler_params=_cparams(("arbitrary", "arbitrary")),
        name="norm_inproj",
    )(x, g, mod4, mod4, w_in, cos, sin)


def _softmax_pv(score_blocks, value_blocks):
    m = functools.reduce(jnp.maximum, [jnp.max(s, axis=-1, keepdims=True) for s in score_blocks])
    ps = [jnp.exp(s - m) for s in score_blocks]
    l = functools.reduce(jnp.add, [jnp.sum(p, axis=-1, keepdims=True) for p in ps])
    o = functools.reduce(jnp.add, [_dot(p.astype(BF16), v) for p, v in zip(ps, value_blocks)])
    return o / l


def _na_kernel(q_ref, kp_ref, kc_ref, kn_ref, vp_ref, vc_ref, vn_ref, kx_ref, vx_ref, bias_ref, o_ref):
    scale = HEAD_DIM ** -0.5
    nq = q_ref.shape[0]
    for h in range(NA_HEADS):
        sl = slice(h * HEAD_DIM, (h + 1) * HEAD_DIM)
        q = q_ref[:, sl]
        scores = []
        for n, k_ref in enumerate((kp_ref, kc_ref, kn_ref)):
            scores.append(_dot_nt(q, k_ref[:, sl]) * scale + bias_ref[h, :, n * nq:(n + 1) * nq])
        scores.append(_dot_nt(q, kx_ref[:, sl]) * scale)
        values = [vp_ref[:, sl], vc_ref[:, sl], vn_ref[:, sl], vx_ref[:, sl]]
        o_ref[:, sl] = _softmax_pv(scores, values).astype(o_ref.dtype)


def na_bias_tables(rel_bias, rows):
    rb = NA_ROWS_PER_BLOCK
    nblk = rows // rb
    qr = jnp.arange(rb)[:, None, None, None]
    qc = jnp.arange(GRID_W)[None, :, None, None]
    kk = jnp.arange(3 * rb)[None, None, :, None]
    kc = jnp.arange(GRID_W)[None, None, None, :]
    cs = jnp.clip(qc - NA_WIN_W // 2, 0, GRID_W - NA_WIN_W)
    col_ok = (kc >= cs) & (kc < cs + NA_WIN_W)
    col_off = jnp.clip(kc - qc + NA_WIN_W - 1, 0, 2 * NA_WIN_W - 2)
    tabs = []
    for j in (0, 1, nblk - 1):
        r = rb * j + qr
        kr = rb * (j - 1) + kk
        rs = jnp.clip(r - NA_WIN_H // 2, 0, rows - NA_WIN_H)
        row_ok = (kr >= rs) & (kr < rs + NA_WIN_H)
        row_off = jnp.clip(kr - r + NA_WIN_H - 1, 0, 2 * NA_WIN_H - 2)
        ok = jnp.broadcast_to(row_ok & col_ok, (rb, GRID_W, 3 * rb, GRID_W))
        ro = jnp.broadcast_to(row_off, ok.shape)
        co = jnp.broadcast_to(col_off, ok.shape)
        t = jnp.where(ok[None], rel_bias[:, ro, co].astype(F32), NEG)
        tabs.append(t.reshape(rel_bias.shape[0], rb * GRID_W, 3 * rb * GRID_W))
    return jnp.stack(tabs)


def neighbourhood_attention(u_x, u_c, bias_tabs, *, batch, seq, ctx_len):
    rows = seq // GRID_W
    assert rows % NA_ROWS_PER_BLOCK == 0 and rows >= 3 * NA_ROWS_PER_BLOCK and rows >= 2 * NA_WIN_H
    nq = NA_ROWS_PER_BLOCK * GRID_W
    nblk = seq // nq
    dn = NA_HEADS * HEAD_DIM

    def blk(col, off):
        def index(b, j):
            return (b * nblk + jnp.clip(j + off, 0, nblk - 1), col)
        return pl.BlockSpec((nq, dn), index)

    def variant(b, j):
        return (jnp.where(j == 0, 0, jnp.where(j == nblk - 1, 2, 1)), 0, 0, 0)

    return pl.pallas_call(
        _na_kernel,
        out_shape=jax.ShapeDtypeStruct((batch * seq, dn), BF16),
        grid=(batch, nblk),
        in_specs=[
            blk(0, 0),
            blk(1, -1), blk(1, 0), blk(1, 1),
            blk(2, -1), blk(2, 0), blk(2, 1),
            pl.BlockSpec((ctx_len, dn), lambda b, j: (b, 1)),
            pl.BlockSpec((ctx_len, dn), lambda b, j: (b, 2)),
            pl.BlockSpec((None, NA_HEADS, nq, 3 * nq), variant),
        ],
        out_specs=pl.BlockSpec((nq, dn), lambda b, j: (b * nblk + j, 0)),
        compiler_params=_cparams(("arbitrary", "arbitrary")),
        name="neighbourhood_attention",
    )(u_x, u_x, u_x, u_x, u_x, u_x, u_x, u_c, u_c, bias_tabs)


def _ctx_attn_kernel(q_ref, k_ref, v_ref, o_ref):
    scale = HEAD_DIM ** -0.5
    for h in range(NA_HEADS):
        sl = slice(h * HEAD_DIM, (h + 1) * HEAD_DIM)
        s = _dot_nt(q_ref[:, sl], k_ref[:, sl]) * scale
        o_ref[:, sl] = _softmax_pv([s], [v_ref[:, sl]]).astype(o_ref.dtype)


def context_attention(u_c, *, batch, ctx_len):
    dn = NA_HEADS * HEAD_DIM
    return pl.pallas_call(
        _ctx_attn_kernel,
        out_shape=jax.ShapeDtypeStruct((batch * ctx_len, dn), BF16),
        grid=(batch,),
        in_specs=[pl.BlockSpec((ctx_len, dn), lambda b, c=c: (b, c)) for c in range(3)],
        out_specs=pl.BlockSpec((ctx_len, dn), lambda b: (b, 0)),
        compiler_params=_cparams(("arbitrary",)),
        name="context_attention",
    )(u_c, u_c, u_c)


def _pool_kernel(up_ref, uc_ref, un_ref, w_ref, ps_ref, o_ref, ext_scr, *, seq):
    i = pl.program_id(1)
    n_i = pl.num_programs(1)
    tm = uc_ref.shape[0]
    n_ext = tm + 2 * POOL_HALO
    ext_scr[0:POOL_HALO, :] = jnp.where(i == 0, 0.0, up_ref[...])
    ext_scr[POOL_HALO:POOL_HALO + tm, :] = uc_ref[...]
    ext_scr[POOL_HALO + tm:n_ext, :] = jnp.where(i == n_i - 1, 0.0, un_ref[...])
    t = i * tm + lax.broadcasted_iota(jnp.int32, (tm, 1), 0)
    for g, w in enumerate(POOL_WINDOWS):
        sl = slice(g * POOL_GROUP, (g + 1) * POOL_GROUP)
        s = ext_scr[:, sl]
        s = s + pltpu.roll(s, 1, axis=0)
        half = 1
        while 2 * half < w:
            s = pltpu.roll(s, half, axis=0) + pltpu.roll(s, n_ext - half, axis=0)
            half *= 2
        count = jnp.minimum(t + w // 2, seq) - jnp.maximum(t - w // 2, 0)
        u = uc_ref[:, sl]
        pooled = s[POOL_HALO:POOL_HALO + tm, :] / count.astype(F32) - u
        y = _dot(pooled.astype(BF16), w_ref[g])
        o_ref[:, sl] = (y * ps_ref[:, sl]).astype(o_ref.dtype)


def multiscale_pool(ub, w_pool, pool_scale, *, batch, seq, tm):
    dp = ub.shape[1]
    n_i = seq // tm
    per8 = tm // POOL_HALO
    n8 = batch * seq // POOL_HALO
    return pl.pallas_call(
        functools.partial(_pool_kernel, seq=seq),
        out_shape=jax.ShapeDtypeStruct((batch * seq, dp), BF16),
        grid=(batch, n_i),
        in_specs=[
            pl.BlockSpec((POOL_HALO, dp), lambda b, i: (jnp.maximum((b * n_i + i) * per8 - 1, 0), 0)),
            pl.BlockSpec((tm, dp), lambda b, i: (b * n_i + i, 0)),
            pl.BlockSpec((POOL_HALO, dp), lambda b, i: (jnp.minimum((b * n_i + i + 1) * per8, n8 - 1), 0)),
            pl.BlockSpec(w_pool.shape, lambda b, i: (0, 0, 0)),
            pl.BlockSpec((1, dp), lambda b, i: (0, 0)),
        ],
        out_specs=pl.BlockSpec((tm, dp), lambda b, i: (b * n_i + i, 0)),
        scratch_shapes=[pltpu.VMEM((tm + 2 * POOL_HALO, dp), F32)],
        compiler_params=_cparams(("arbitrary", "arbitrary")),
        name="multiscale_pool",
    )(ub, ub, ub, w_pool, pool_scale)


def _da_kernel(lam_ref, q_ref, k_ref, v_ref, *rest, has_ctx, lam_init):
    if has_ctx:
        kx_ref, vx_ref, g_ref, o_ref, m_scr, l_scr, acc_scr = rest
    else:
        g_ref, o_ref, m_scr, l_scr, acc_scr = rest
    kj = pl.program_id(3)
    scale = HEAD_DIM ** -0.5

    @pl.when(kj == 0)
    def _():
        m_scr[...] = jnp.full(m_scr.shape, NEG, F32)
        l_scr[...] = jnp.zeros(l_scr.shape, F32)
        acc_scr[...] = jnp.zeros(acc_scr.shape, F32)

    def update(k_blk, v_blk):
        for c in range(2):
            sl = slice(c * HEAD_DIM, (c + 1) * HEAD_DIM)
            s = _dot_nt(q_ref[:, sl], k_blk[:, sl]) * scale
            m_old = m_scr[c]
            m_new = jnp.maximum(m_old, jnp.max(s, axis=-1, keepdims=True))
            alpha = jnp.exp(m_old - m_new)
            p = jnp.exp(s - m_new)
            l_scr[c] = alpha * l_scr[c] + jnp.sum(p, axis=-1, keepdims=True)
            acc_scr[c] = alpha * acc_scr[c] + _dot(p.astype(BF16), v_blk)
            m_scr[c] = m_new

    if has_ctx:
        @pl.when(kj == 0)
        def _():
            update(kx_ref[...], vx_ref[...])

    update(k_ref[...], v_ref[...])

    @pl.when(kj == pl.num_programs(3) - 1)
    def _():
        lv = lam_ref[...]
        lam = (jnp.exp(jnp.sum(lv[0:1] * lv[1:2], axis=-1, keepdims=True))
               - jnp.exp(jnp.sum(lv[2:3] * lv[3:4], axis=-1, keepdims=True)) + lam_init)
        o = acc_scr[0] / l_scr[0] - lam * (acc_scr[1] / l_scr[1])
        y = o * lax.rsqrt(jnp.mean(o * o, axis=-1, keepdims=True) + EPS)
        o_ref[...] = (y * g_ref[...] * (1.0 - lam_init)).astype(o_ref.dtype)


def differential_attention(u_q, u_ctx, lam_vecs, g_subln, lam_init, *, batch, seq, ctx_len, tq, tk):
    has_ctx = u_ctx is not None
    nq, nk = seq // tq, seq // tk
    qcol, kcol, vcol = 2048 // DA_V_DIM, 3072 // DA_V_DIM, 4096 // DA_V_DIM
    in_specs = [
        pl.BlockSpec((4, HEAD_DIM), lambda b, h, i, j: (0, 0)),
        pl.BlockSpec((tq, DA_V_DIM), lambda b, h, i, j: (b * nq + i, qcol + h)),
        pl.BlockSpec((tk, DA_V_DIM), lambda b, h, i, j: (b * nk + j, kcol + h)),
        pl.BlockSpec((tk, DA_V_DIM), lambda b, h, i, j: (b * nk + j, vcol + h)),
    ]
    args = [lam_vecs, u_q, u_q, u_q]
    if has_ctx:
        in_specs += [
            pl.BlockSpec((ctx_len, DA_V_DIM), lambda b, h, i, j: (b, kcol + h)),
            pl.BlockSpec((ctx_len, DA_V_DIM), lambda b, h, i, j: (b, vcol + h)),
        ]
        args += [u_ctx, u_ctx]
    in_specs.append(pl.BlockSpec((1, DA_V_DIM), lambda b, h, i, j: (0, 0)))
    args.append(g_subln)
    return pl.pallas_call(
        functools.partial(_da_kernel, has_ctx=has_ctx, lam_init=lam_init),
        out_shape=jax.ShapeDtypeStruct((batch * seq, DA_HEADS * DA_V_DIM), BF16),
        grid=(batch, DA_HEADS, nq, nk),
        in_specs=in_specs,
        out_specs=pl.BlockSpec((tq, DA_V_DIM), lambda b, h, i, j: (b * nq + i, h)),
        scratch_shapes=[pltpu.VMEM((2, tq, 1), F32), pltpu.VMEM((2, tq, 1), F32),
                        pltpu.VMEM((2, tq, DA_V_DIM), F32)],
        compiler_params=_cparams(("arbitrary",) * 4),
        name="differential_attention",
    )(*args)


def _outproj_kernel(oa_ref, ob_ref, oc_ref, w_ref, x_ref, ga_ref, g_ref, sh_ref, sc_ref, xo_ref, h_ref):
    na, nb = oa_ref.shape[1], ob_ref.shape[1]
    y = (_dot(oa_ref[...], w_ref[0:na, :]) + _dot(ob_ref[...], w_ref[na:na + nb, :])
         + _dot(oc_ref[...], w_ref[na + nb:, :]))
    x = x_ref[...] + ga_ref[...] * y
    xo_ref[...] = x
    z = x * lax.rsqrt(jnp.mean(x * x, axis=-1, keepdims=True) + EPS)
    h_ref[...] = (z * (g_ref[...] * (1.0 + sc_ref[...])) + sh_ref[...]).astype(BF16)


def outproj_residual(oa, ob, oc, w_out, x, g_ffn, mod4, layer, mod_rows, *, tm, seq):
    r, d = x.shape
    tiles_per_seq = seq // tm

    def mod_spec(chunk):
        return pl.BlockSpec((None, None, 1, d),
                            lambda i: (layer, mod_rows(i // tiles_per_seq), 0, chunk))

    def row_spec(a):
        return pl.BlockSpec((tm, a.shape[1]), lambda i: (i, 0))

    return pl.pallas_call(
        _outproj_kernel,
        out_shape=(jax.ShapeDtypeStruct((r, d), F32), jax.ShapeDtypeStruct((r, d), BF16)),
        grid=(r // tm,),
        in_specs=[row_spec(oa), row_spec(ob), row_spec(oc),
                  pl.BlockSpec(w_out.shape, lambda i: (0, 0)),
                  row_spec(x), mod_spec(2),
                  pl.BlockSpec((1, d), lambda i: (0, 0)), mod_spec(3), mod_spec(4)],
        out_specs=(pl.BlockSpec((tm, d), lambda i: (i, 0)), pl.BlockSpec((tm, d), lambda i: (i, 0))),
        compiler_params=_cparams(("arbitrary",)),
        name="outproj_residual",
    )(oa, ob, oc, w_out, x, mod4, g_ffn, mod4, mod4)


def _mlp_up_kernel(hp_ref, hc_ref, hn_ref, wg_ref, wu_ref, cwg_ref, cwu_ref, cbg_ref, cbu_ref, o_ref, lhs_scr,
                   *, tiles_per_seq):
    i = pl.program_id(0)
    j = pl.program_id(1)
    tm = hc_ref.shape[0]
    n_ext = tm + 2 * CONV_HALO

    @pl.when(j == 0)
    def _():
        first = (i % tiles_per_seq) == 0
        last = (i % tiles_per_seq) == tiles_per_seq - 1
        lhs_scr[0:CONV_HALO, :] = jnp.where(first, jnp.zeros_like(hp_ref), hp_ref[...])
        lhs_scr[CONV_HALO:CONV_HALO + tm, :] = hc_ref[...]
        lhs_scr[CONV_HALO + tm:n_ext, :] = jnp.where(last, jnp.zeros_like(hn_ref), hn_ref[...])

    lhs = lhs_scr[...]

    def conv(w_ref, cw_ref, cb_ref):
        a = _dot(lhs, w_ref[...])
        cw = cw_ref[...]
        c = (pltpu.roll(a, 1, axis=0) * cw[0:1] + a * cw[1:2] + pltpu.roll(a, n_ext - 1, axis=0) * cw[2:3]
             + cb_ref[...])
        return c[CONV_HALO:CONV_HALO + tm, :]

    gate = conv(wg_ref, cwg_ref, cbg_ref)
    up = conv(wu_ref, cwu_ref, cbu_ref)
    o_ref[...] = (gate * jax.nn.sigmoid(gate) * up).astype(o_ref.dtype)


def mlp_up(h, w_up, conv_w, conv_b, *, tm, seq):
    r, d = h.shape
    f = w_up.shape[1] // 2
    fc = 512
    nf = f // fc
    per16 = tm // CONV_HALO
    n16 = r // CONV_HALO
    tiles_per_seq = seq // tm
    return pl.pallas_call(
        functools.partial(_mlp_up_kernel, tiles_per_seq=tiles_per_seq),
        out_shape=jax.ShapeDtypeStruct((r, f), BF16),
        grid=(r // tm, nf),
        in_specs=[
            pl.BlockSpec((CONV_HALO, d), lambda i, j: (jnp.maximum(i * per16 - 1, 0), 0)),
            pl.BlockSpec((tm, d), lambda i, j: (i, 0)),
            pl.BlockSpec((CONV_HALO, d), lambda i, j: (jnp.minimum((i + 1) * per16, n16 - 1), 0)),
            pl.BlockSpec((d, fc), lambda i, j: (0, j)),
            pl.BlockSpec((d, fc), lambda i, j: (0, nf + j)),
            pl.BlockSpec((3, fc), lambda i, j: (0, j)),
            pl.BlockSpec((3, fc), lambda i, j: (0, nf + j)),
            pl.BlockSpec((1, fc), lambda i, j: (0, j)),
            pl.BlockSpec((1, fc), lambda i, j: (0, nf + j)),
        ],
        out_specs=pl.BlockSpec((tm, fc), lambda i, j: (i, j)),
        scratch_shapes=[pltpu.VMEM((tm + 2 * CONV_HALO, d), BF16)],
        compiler_params=_cparams(("arbitrary", "arbitrary")),
        name="mlp_up_conv_gate",
    )(h, h, h, w_up, w_up, conv_w, conv_w, conv_b, conv_b)


def _mlp_down_kernel(g_ref, w_ref, x_ref, gf_ref, gfin_ref, o_ref, acc_scr, *, final_norm):
    k = pl.program_id(1)

    @pl.when(k == 0)
    def _():
        acc_scr[...] = jnp.zeros(acc_scr.shape, F32)

    acc_scr[...] += _dot(g_ref[...], w_ref[...])

    @pl.when(k == pl.num_programs(1) - 1)
    def _():
        x = x_ref[...] + gf_ref[...] * acc_scr[...]
        if final_norm:
            x = x * lax.rsqrt(jnp.mean(x * x, axis=-1, keepdims=True) + EPS) * gfin_ref[...]
        o_ref[...] = x


def mlp_down_residual(g, w_down, x, mod4, layer, mod_rows, g_final, *, tm, seq, final_norm):
    r, d = x.shape
    f = g.shape[1]
    kc = 512
    tiles_per_seq = seq // tm
    return pl.pallas_call(
        functools.partial(_mlp_down_kernel, final_norm=final_norm),
        out_shape=jax.ShapeDtypeStruct((r, d), F32),
        grid=(r // tm, f // kc),
        in_specs=[
            pl.BlockSpec((tm, kc), lambda i, k: (i, k)),
            pl.BlockSpec((kc, d), lambda i, k: (k, 0)),
            pl.BlockSpec((tm, d), lambda i, k: (i, 0)),
            pl.BlockSpec((None, None, 1, d), lambda i, k: (layer, mod_rows(i // tiles_per_seq), 0, 5)),
            pl.BlockSpec((1, d), lambda i, k: (0, 0)),
        ],
        out_specs=pl.BlockSpec((tm, d), lambda i, k: (i, 0)),
        scratch_shapes=[pltpu.VMEM((tm, d), F32)],
        compiler_params=_cparams(("arbitrary", "arbitrary")),
        name="mlp_down_residual",
    )(g, w_down, x, mod4, g_final)


def rope_tables(n_tokens):
    t = jnp.arange(n_tokens)
    n_freq = HEAD_DIM // 4
    inv = ROPE_BASE ** (-jnp.arange(n_freq, dtype=F32) / n_freq)
    ang_r = (t // GRID_W).astype(F32)[:, None] * inv
    ang_c = (t % GRID_W).astype(F32)[:, None] * inv
    cos = jnp.concatenate([jnp.cos(ang_r), jnp.cos(ang_r), jnp.cos(ang_c), jnp.cos(ang_c)], axis=-1)
    sin = jnp.concatenate([-jnp.sin(ang_r), jnp.sin(ang_r), -jnp.sin(ang_c), jnp.sin(ang_c)], axis=-1)
    return cos, sin


def kernel(x, c, ctx, c_ctx, w_mod, b_mod, g_mix, w_in, na_bias, w_pool, pool_scale, lambda_q1, lambda_k1,
           lambda_q2, lambda_k2, g_subln, w_out, g_ffn, w_up, conv_w, conv_b, w_down, g_final):
    batch, seq, d = x.shape
    ctx_len = ctx.shape[1]
    depth = w_mod.shape[0]
    rows = seq // GRID_W

    cond_t = jnp.zeros((d, 8), F32).at[:, :batch].set(c.T).at[:, batch].set(c_ctx)
    mod = ada_modulation(cond_t, w_mod, b_mod, batch + 1)
    mod4 = mod.reshape(depth, 8, 1, 6 * d)
    x_rows = lambda s: s
    ctx_rows = lambda s: batch

    cos_x, sin_x = rope_tables(seq)
    cos_c, sin_c = jnp.ones((ctx_len, HEAD_DIM), F32), jnp.zeros((ctx_len, HEAD_DIM), F32)

    w_in_b, w_out_b = w_in.astype(BF16), w_out.astype(BF16)
    w_up_b, w_down_b, w_pool_b = w_up.astype(BF16), w_down.astype(BF16), w_pool.astype(BF16)

    xs = x.reshape(batch * seq, d)
    cs = ctx.reshape(batch * ctx_len, d)
    tm_c = ctx_len
    for layer in range(depth):
        last = layer == depth - 1
        lam_init = 0.8 - 0.6 * math.exp(-0.3 * layer)
        lam_vecs = jnp.stack([lambda_q1[layer], lambda_k1[layer], lambda_q2[layer], lambda_k2[layer]])
        g_mix_l, g_ffn_l = g_mix[layer][None], g_ffn[layer][None]
        g_sub_l, ps_l = g_subln[layer][None], pool_scale[layer][None]
        bias_tabs = na_bias_tables(na_bias[layer], rows)

        u_x, ub_x = norm_inproj(xs, g_mix_l, mod4, layer, x_rows, w_in_b[layer], cos_x, sin_x, tm=1024, seq=seq)
        u_c, ub_c = norm_inproj(cs, g_mix_l, mod4, layer, ctx_rows, w_in_b[layer], cos_c, sin_c,
                                tm=tm_c, seq=ctx_len)

        oa_x = neighbourhood_attention(u_x, u_c, bias_tabs, batch=batch, seq=seq, ctx_len=ctx_len)
        ob_x = multiscale_pool(ub_x, w_pool_b[layer], ps_l, batch=batch, seq=seq, tm=512)
        oc_x = differential_attention(u_x, u_c, lam_vecs, g_sub_l, lam_init, batch=batch, seq=seq,
                                      ctx_len=ctx_len, tq=512, tk=512)
        x_new, h_x = outproj_residual(oa_x, ob_x, oc_x, w_out_b[layer], xs, g_ffn_l, mod4, layer, x_rows,
                                      tm=512, seq=seq)
        g_x = mlp_up(h_x, w_up_b[layer], conv_w[layer], conv_b[layer][None], tm=1024, seq=seq)
        xs = mlp_down_residual(g_x, w_down_b[layer], x_new, mod4, layer, x_rows, g_final[None], tm=512, seq=seq,
                               final_norm=last)

        if not last:
            oa_c = context_attention(u_c, batch=batch, ctx_len=ctx_len)
            ob_c = multiscale_pool(ub_c, w_pool_b[layer], ps_l, batch=batch, seq=ctx_len, tm=tm_c)
            oc_c = differential_attention(u_c, None, lam_vecs, g_sub_l, lam_init, batch=batch, seq=ctx_len,
                                          ctx_len=ctx_len, tq=tm_c, tk=tm_c)
            c_new, h_c = outproj_residual(oa_c, ob_c, oc_c, w_out_b[layer], cs, g_ffn_l, mod4, layer, ctx_rows,
                                          tm=tm_c, seq=ctx_len)
            g_c = mlp_up(h_c, w_up_b[layer], conv_w[layer], conv_b[layer][None], tm=tm_c, seq=ctx_len)
            cs = mlp_down_residual(g_c, w_down_b[layer], c_new, mod4, layer, ctx_rows, g_final[None], tm=tm_c,
                                   seq=ctx_len, final_norm=False)
    return xs.reshape(batch, seq, d)
```

```python
import functools
import math

import jax
import jax.numpy as jnp
import numpy as np
from jax import lax
from jax.experimental import pallas as pl
from jax.experimental.pallas import tpu as pltpu

F32 = jnp.float32
BF16 = jnp.bfloat16

GRID_W = 64
HEAD_DIM = 128
NA_HEADS = 4
NA_WIN_H = 8
NA_WIN_W = 16
NA_ROWS_PER_BLOCK = 4
POOL_WINDOWS = (2, 4, 8, 16)
POOL_GROUP = 128
POOL_HALO = 8
DA_HEADS = 4
DA_V_DIM = 256
CONV_HALO = 16
ROPE_BASE = 10000.0
EPS = 1e-6
NEG = -1e30
DA_Q_SCALE = HEAD_DIM ** -0.5 * math.log2(math.e)
VMEM_LIMIT = 56 * 1024 * 1024


def _cparams(sem):
    return pltpu.CompilerParams(dimension_semantics=sem, vmem_limit_bytes=VMEM_LIMIT)


def _dot(a, b):
    return jnp.dot(a, b, preferred_element_type=F32)


def _dot_nt(a, b):
    return lax.dot_general(a, b, (((1,), (1,)), ((), ())), preferred_element_type=F32)


def _dot_tn(a, b):
    return lax.dot_general(a, b, (((0,), (0,)), ((), ())), preferred_element_type=F32)


def _mod_kernel(ct_ref, w_ref, b_ref, o_ref, *, n_cond):
    ct = ct_ref[...]
    a = ct * jax.nn.sigmoid(ct)
    w = w_ref[...]
    rows = [jnp.sum(a[:, r:r + 1] * w, axis=0, keepdims=True) for r in range(n_cond)]
    rows.append(jnp.zeros((8 - n_cond, w.shape[1]), F32))
    o_ref[...] = jnp.concatenate(rows, axis=0) + b_ref[...]


def ada_modulation(cond_t, w_mod, b_mod, n_cond):
    depth, d, n = w_mod.shape
    tn = 1024
    return pl.pallas_call(
        functools.partial(_mod_kernel, n_cond=n_cond),
        out_shape=jax.ShapeDtypeStruct((depth, 8, n), F32),
        grid=(depth, n // tn),
        in_specs=[
            pl.BlockSpec((d, 8), lambda l, j: (0, 0)),
            pl.BlockSpec((None, d, tn), lambda l, j: (l, 0, j)),
            pl.BlockSpec((None, 1, tn), lambda l, j: (l, 0, j)),
        ],
        out_specs=pl.BlockSpec((None, 8, tn), lambda l, j: (l, 0, j)),
        compiler_params=_cparams(("arbitrary", "arbitrary")),
        name="ada_modulation",
    )(cond_t, w_mod, b_mod.reshape(depth, 1, n))


def _swap32(x):
    lane = lax.broadcasted_iota(jnp.int32, x.shape, 1)
    return jnp.where((lane % 64) < 32, pltpu.roll(x, 96, axis=1), pltpu.roll(x, 32, axis=1))


def _inproj_kernel(x_ref, g_ref, sh_ref, sc_ref, w_ref, cos_ref, sin_ref, u_ref, ub_ref, h_scr,
                   *, pool_block, rope_blocks):
    j = pl.program_id(1)

    @pl.when(j == 0)
    def _():
        x = x_ref[...]
        y = x * lax.rsqrt(jnp.mean(x * x, axis=-1, keepdims=True) + EPS)
        h = y * (g_ref[...] * (1.0 + sc_ref[...])) + sh_ref[...]
        h_scr[...] = h.astype(BF16)

    acc = _dot(h_scr[...], w_ref[...])

    is_rope = (j >= rope_blocks[0]) & (j < rope_blocks[1])

    @pl.when(is_rope)
    def _():
        qk_scale = jnp.where(j < rope_blocks[2], DA_Q_SCALE, 1.0)
        cos = cos_ref[...] * qk_scale
        sin = sin_ref[...] * qk_scale
        for c in range(acc.shape[1] // HEAD_DIM):
            xc = acc[:, c * HEAD_DIM:(c + 1) * HEAD_DIM]
            u_ref[:, c * HEAD_DIM:(c + 1) * HEAD_DIM] = (xc * cos + _swap32(xc) * sin).astype(BF16)

    @pl.when(jnp.logical_not(is_rope))
    def _():
        u_ref[...] = acc.astype(BF16)

    @pl.when(j == pool_block)
    def _():
        ub_ref[...] = acc


def norm_inproj(x, g, mod4, layer, mod_rows, w_in, cos, sin, *, tm, seq):
    r, d = x.shape
    d_in = w_in.shape[1]
    tn = 512
    tiles_per_seq = seq // tm
    pool_block = 1536 // tn
    rope_blocks = (2048 // tn, 4096 // tn, 3072 // tn)

    def mod_spec(chunk):
        return pl.BlockSpec((None, None, 1, d),
                            lambda i, j: (layer, mod_rows(i // tiles_per_seq), 0, chunk))

    return pl.pallas_call(
        functools.partial(_inproj_kernel, pool_block=pool_block, rope_blocks=rope_blocks),
        out_shape=(jax.ShapeDtypeStruct((r, d_in), BF16), jax.ShapeDtypeStruct((r, 512), F32)),
        grid=(r // tm, d_in // tn),
        in_specs=[
            pl.BlockSpec((tm, d), lambda i, j: (i, 0)),
            pl.BlockSpec((1, d), lambda i, j: (0, 0)),
            mod_spec(0),
            mod_spec(1),
            pl.BlockSpec((d, tn), lambda i, j: (0, j)),
            pl.BlockSpec((tm, HEAD_DIM), lambda i, j: (i % tiles_per_seq, 0)),
            pl.BlockSpec((tm, HEAD_DIM), lambda i, j: (i % tiles_per_seq, 0)),
        ],
        out_specs=(pl.BlockSpec((tm, tn), lambda i, j: (i, j)),
                   pl.BlockSpec((tm, 512), lambda i, j: (i, 0))),
        scratch_shapes=[pltpu.VMEM((tm, d), BF16)],
        compiler_params=_cparams(("arbitrary", "arbitrary")),
        name="norm_inproj",
    )(x, g, mod4, mod4, w_in, cos, sin)


def _softmax_pv(score_blocks, value_blocks):
    m = functools.reduce(jnp.maximum, [jnp.max(s, axis=-1, keepdims=True) for s in score_blocks])
    ps = [jnp.exp(s - m) for s in score_blocks]
    l = functools.reduce(jnp.add, [jnp.sum(p, axis=-1, keepdims=True) for p in ps])
    o = functools.reduce(jnp.add, [_dot(p.astype(BF16), v) for p, v in zip(ps, value_blocks)])
    return o / l


def _na_kernel(q_ref, kp_ref, kc_ref, kn_ref, vp_ref, vc_ref, vn_ref, kx_ref, vx_ref, bias_ref, o_ref):
    scale = HEAD_DIM ** -0.5
    nq = q_ref.shape[0]
    for h in range(NA_HEADS):
        sl = slice(h * HEAD_DIM, (h + 1) * HEAD_DIM)
        q = q_ref[:, sl]
        scores = []
        for n, k_ref in enumerate((kp_ref, kc_ref, kn_ref)):
            scores.append(_dot_nt(q, k_ref[:, sl]) * scale + bias_ref[h, :, n * nq:(n + 1) * nq])
        scores.append(_dot_nt(q, kx_ref[:, sl]) * scale)
        values = [vp_ref[:, sl], vc_ref[:, sl], vn_ref[:, sl], vx_ref[:, sl]]
        o_ref[:, sl] = _softmax_pv(scores, values).astype(o_ref.dtype)


def na_bias_tables(rel_bias, rows):
    rb = NA_ROWS_PER_BLOCK
    nblk = rows // rb
    qr = np.arange(rb)[:, None, None, None]
    qc = np.arange(GRID_W)[None, :, None, None]
    kk = np.arange(3 * rb)[None, None, :, None]
    kc = np.arange(GRID_W)[None, None, None, :]
    cs = np.clip(qc - NA_WIN_W // 2, 0, GRID_W - NA_WIN_W)
    col_ok = (kc >= cs) & (kc < cs + NA_WIN_W)
    ok = []
    for j in (0, 1, nblk - 1):
        r = rb * j + qr
        kr = rb * (j - 1) + kk
        rs = np.clip(r - NA_WIN_H // 2, 0, rows - NA_WIN_H)
        ok.append(np.broadcast_to((kr >= rs) & (kr < rs + NA_WIN_H) & col_ok, (rb, GRID_W, 3 * rb, GRID_W)))
    ok = np.stack(ok)[:, None]
    row_off = (kk - qr - rb + NA_WIN_H - 1)[:, 0, :, 0]
    col_off = (kc - qc + NA_WIN_W - 1)[0, :, 0, :]
    rsel = (row_off[None] == np.arange(2 * NA_WIN_H - 1)[:, None, None]).astype(np.float32)
    csel = (col_off[None] == np.arange(2 * NA_WIN_W - 1)[:, None, None]).astype(np.float32)
    rel = jnp.einsum("hab,aqk,bcd->hqckd", rel_bias.astype(F32), rsel, csel, precision=lax.Precision.HIGHEST)
    tabs = jnp.where(ok, rel[None], NEG)
    return tabs.reshape(3, rel_bias.shape[0], rb * GRID_W, 3 * rb * GRID_W)


def neighbourhood_attention(u_x, u_c, bias_tabs, *, batch, seq, ctx_len):
    rows = seq // GRID_W
    assert rows % NA_ROWS_PER_BLOCK == 0 and rows >= 3 * NA_ROWS_PER_BLOCK and rows >= 2 * NA_WIN_H
    nq = NA_ROWS_PER_BLOCK * GRID_W
    nblk = seq // nq
    dn = NA_HEADS * HEAD_DIM

    def blk(col, off):
        def index(b, j):
            return (b * nblk + jnp.clip(j + off, 0, nblk - 1), col)
        return pl.BlockSpec((nq, dn), index)

    def variant(b, j):
        return (jnp.where(j == 0, 0, jnp.where(j == nblk - 1, 2, 1)), 0, 0, 0)

    return pl.pallas_call(
        _na_kernel,
        out_shape=jax.ShapeDtypeStruct((batch * seq, dn), BF16),
        grid=(batch, nblk),
        in_specs=[
            blk(0, 0),
            blk(1, -1), blk(1, 0), blk(1, 1),
            blk(2, -1), blk(2, 0), blk(2, 1),
            pl.BlockSpec((ctx_len, dn), lambda b, j: (b, 1)),
            pl.BlockSpec((ctx_len, dn), lambda b, j: (b, 2)),
            pl.BlockSpec((None, NA_HEADS, nq, 3 * nq), variant),
        ],
        out_specs=pl.BlockSpec((nq, dn), lambda b, j: (b * nblk + j, 0)),
        compiler_params=_cparams(("arbitrary", "arbitrary")),
        name="neighbourhood_attention",
    )(u_x, u_x, u_x, u_x, u_x, u_x, u_x, u_c, u_c, bias_tabs)


def _ctx_attn_kernel(q_ref, k_ref, v_ref, o_ref):
    scale = HEAD_DIM ** -0.5
    for h in range(NA_HEADS):
        sl = slice(h * HEAD_DIM, (h + 1) * HEAD_DIM)
        s = _dot_nt(q_ref[:, sl], k_ref[:, sl]) * scale
        o_ref[:, sl] = _softmax_pv([s], [v_ref[:, sl]]).astype(o_ref.dtype)


def context_attention(u_c, *, batch, ctx_len):
    dn = NA_HEADS * HEAD_DIM
    return pl.pallas_call(
        _ctx_attn_kernel,
        out_shape=jax.ShapeDtypeStruct((batch * ctx_len, dn), BF16),
        grid=(batch,),
        in_specs=[pl.BlockSpec((ctx_len, dn), lambda b, c=c: (b, c)) for c in range(3)],
        out_specs=pl.BlockSpec((ctx_len, dn), lambda b: (b, 0)),
        compiler_params=_cparams(("arbitrary",)),
        name="context_attention",
    )(u_c, u_c, u_c)


def _pool_kernel(up_ref, uc_ref, un_ref, w_ref, ps_ref, o_ref, ext_scr, *, seq):
    i = pl.program_id(1)
    n_i = pl.num_programs(1)
    tm = uc_ref.shape[0]
    n_ext = tm + 2 * POOL_HALO
    ext_scr[0:POOL_HALO, :] = jnp.where(i == 0, 0.0, up_ref[...])
    ext_scr[POOL_HALO:POOL_HALO + tm, :] = uc_ref[...]
    ext_scr[POOL_HALO + tm:n_ext, :] = jnp.where(i == n_i - 1, 0.0, un_ref[...])
    t = i * tm + lax.broadcasted_iota(jnp.int32, (tm, 1), 0)
    for g, w in enumerate(POOL_WINDOWS):
        sl = slice(g * POOL_GROUP, (g + 1) * POOL_GROUP)
        s = ext_scr[:, sl]
        s = s + pltpu.roll(s, 1, axis=0)
        half = 1
        while 2 * half < w:
            s = pltpu.roll(s, half, axis=0) + pltpu.roll(s, n_ext - half, axis=0)
            half *= 2
        count = jnp.minimum(t + w // 2, seq) - jnp.maximum(t - w // 2, 0)
        u = uc_ref[:, sl]
        pooled = s[POOL_HALO:POOL_HALO + tm, :] / count.astype(F32) - u
        y = _dot(pooled.astype(BF16), w_ref[g])
        o_ref[:, sl] = (y * ps_ref[:, sl]).astype(o_ref.dtype)


def multiscale_pool(ub, w_pool, pool_scale, *, batch, seq, tm):
    dp = ub.shape[1]
    n_i = seq // tm
    per8 = tm // POOL_HALO
    n8 = batch * seq // POOL_HALO
    return pl.pallas_call(
        functools.partial(_pool_kernel, seq=seq),
        out_shape=jax.ShapeDtypeStruct((batch * seq, dp), BF16),
        grid=(batch, n_i),
        in_specs=[
            pl.BlockSpec((POOL_HALO, dp), lambda b, i: (jnp.maximum((b * n_i + i) * per8 - 1, 0), 0)),
            pl.BlockSpec((tm, dp), lambda b, i: (b * n_i + i, 0)),
            pl.BlockSpec((POOL_HALO, dp), lambda b, i: (jnp.minimum((b * n_i + i + 1) * per8, n8 - 1), 0)),
            pl.BlockSpec(w_pool.shape, lambda b, i: (0, 0, 0)),
            pl.BlockSpec((1, dp), lambda b, i: (0, 0)),
        ],
        out_specs=pl.BlockSpec((tm, dp), lambda b, i: (b * n_i + i, 0)),
        scratch_shapes=[pltpu.VMEM((tm + 2 * POOL_HALO, dp), F32)],
        compiler_params=_cparams(("arbitrary", "arbitrary")),
        name="multiscale_pool",
    )(ub, ub, ub, w_pool, pool_scale)


def _da_kernel(lam_ref, q_ref, k_ref, v_ref, *rest, has_ctx, lam_init):
    if has_ctx:
        kx_ref, vx_ref, g_ref, o_ref, m_scr, l_scr, acc_scr = rest
    else:
        g_ref, o_ref, m_scr, l_scr, acc_scr = rest
    kj = pl.program_id(3)

    @pl.when(kj == 0)
    def _():
        m_scr[...] = jnp.full(m_scr.shape, NEG, F32)
        l_scr[...] = jnp.zeros(l_scr.shape, F32)
        acc_scr[...] = jnp.zeros(acc_scr.shape, F32)

    def update(k_blk_ref, v_blk_ref):
        v = v_blk_ref[...]
        for c in range(2):
            sl = slice(c * HEAD_DIM, (c + 1) * HEAD_DIM)
            s = _dot_nt(k_blk_ref[:, sl], q_ref[:, sl])
            m_old = m_scr[c]
            m_new = jnp.maximum(m_old, jnp.max(s, axis=0, keepdims=True))
            alpha = jnp.exp2(m_old - m_new)
            p = jnp.exp2(s - m_new)
            l_scr[c] = alpha * l_scr[c] + jnp.sum(p, axis=0, keepdims=True)
            acc_scr[c] = alpha * acc_scr[c] + _dot_tn(v, p.astype(BF16))
            m_scr[c] = m_new

    if has_ctx:
        @pl.when(kj == 0)
        def _():
            update(kx_ref, vx_ref)

    update(k_ref, v_ref)

    @pl.when(kj == pl.num_programs(3) - 1)
    def _():
        lv = lam_ref[...]
        lam = (jnp.exp(jnp.sum(lv[0:1] * lv[1:2], axis=-1, keepdims=True))
               - jnp.exp(jnp.sum(lv[2:3] * lv[3:4], axis=-1, keepdims=True)) + lam_init)
        o = acc_scr[0] / l_scr[0] - lam * (acc_scr[1] / l_scr[1])
        y = o * lax.rsqrt(jnp.mean(o * o, axis=0, keepdims=True) + EPS)
        o_ref[...] = (y.T * g_ref[...] * (1.0 - lam_init)).astype(o_ref.dtype)


def differential_attention(u_q, u_ctx, lam_vecs, g_subln, lam_init, *, batch, seq, ctx_len, tq, tk):
    has_ctx = u_ctx is not None
    nq, nk = seq // tq, seq // tk
    qcol, kcol, vcol = 2048 // DA_V_DIM, 3072 // DA_V_DIM, 4096 // DA_V_DIM
    in_specs = [
        pl.BlockSpec((4, HEAD_DIM), lambda b, h, i, j: (0, 0)),
        pl.BlockSpec((tq, DA_V_DIM), lambda b, h, i, j: (b * nq + i, qcol + h)),
        pl.BlockSpec((tk, DA_V_DIM), lambda b, h, i, j: (b * nk + j, kcol + h)),
        pl.BlockSpec((tk, DA_V_DIM), lambda b, h, i, j: (b * nk + j, vcol + h)),
    ]
    args = [lam_vecs, u_q, u_q, u_q]
    if has_ctx:
        in_specs += [
            pl.BlockSpec((ctx_len, DA_V_DIM), lambda b, h, i, j: (b, kcol + h)),
            pl.BlockSpec((ctx_len, DA_V_DIM), lambda b, h, i, j: (b, vcol + h)),
        ]
        args += [u_ctx, u_ctx]
    in_specs.append(pl.BlockSpec((1, DA_V_DIM), lambda b, h, i, j: (0, 0)))
    args.append(g_subln)
    return pl.pallas_call(
        functools.partial(_da_kernel, has_ctx=has_ctx, lam_init=lam_init),
        out_shape=jax.ShapeDtypeStruct((batch * seq, DA_HEADS * DA_V_DIM), BF16),
        grid=(batch, DA_HEADS, nq, nk),
        in_specs=in_specs,
        out_specs=pl.BlockSpec((tq, DA_V_DIM), lambda b, h, i, j: (b * nq + i, h)),
        scratch_shapes=[pltpu.VMEM((2, 1, tq), F32), pltpu.VMEM((2, 1, tq), F32),
                        pltpu.VMEM((2, DA_V_DIM, tq), F32)],
        compiler_params=_cparams(("arbitrary",) * 4),
        name="differential_attention",
    )(*args)


def _outproj_kernel(oa_ref, ob_ref, oc_ref, w_ref, x_ref, ga_ref, g_ref, sh_ref, sc_ref, xo_ref, h_ref):
    na, nb = oa_ref.shape[1], ob_ref.shape[1]
    y = (_dot(oa_ref[...], w_ref[0:na, :]) + _dot(ob_ref[...], w_ref[na:na + nb, :])
         + _dot(oc_ref[...], w_ref[na + nb:, :]))
    x = x_ref[...] + ga_ref[...] * y
    xo_ref[...] = x
    z = x * lax.rsqrt(jnp.mean(x * x, axis=-1, keepdims=True) + EPS)
    h_ref[...] = (z * (g_ref[...] * (1.0 + sc_ref[...])) + sh_ref[...]).astype(BF16)


def outproj_residual(oa, ob, oc, w_out, x, g_ffn, mod4, layer, mod_rows, *, tm, seq):
    r, d = x.shape
    tiles_per_seq = seq // tm

    def mod_spec(chunk):
        return pl.BlockSpec((None, None, 1, d),
                            lambda i: (layer, mod_rows(i // tiles_per_seq), 0, chunk))

    def row_spec(a):
        return pl.BlockSpec((tm, a.shape[1]), lambda i: (i, 0))

    return pl.pallas_call(
        _outproj_kernel,
        out_shape=(jax.ShapeDtypeStruct((r, d), F32), jax.ShapeDtypeStruct((r, d), BF16)),
        grid=(r // tm,),
        in_specs=[row_spec(oa), row_spec(ob), row_spec(oc),
                  pl.BlockSpec(w_out.shape, lambda i: (0, 0)),
                  row_spec(x), mod_spec(2),
                  pl.BlockSpec((1, d), lambda i: (0, 0)), mod_spec(3), mod_spec(4)],
        out_specs=(pl.BlockSpec((tm, d), lambda i: (i, 0)), pl.BlockSpec((tm, d), lambda i: (i, 0))),
        compiler_params=_cparams(("arbitrary",)),
        name="outproj_residual",
    )(oa, ob, oc, w_out, x, mod4, g_ffn, mod4, mod4)


def _mlp_up_kernel(hp_ref, hc_ref, hn_ref, wg_ref, wu_ref, cwg_ref, cwu_ref, cbg_ref, cbu_ref, o_ref, lhs_scr,
                   *, tiles_per_seq):
    i = pl.program_id(0)
    j = pl.program_id(1)
    tm = hc_ref.shape[0]
    n_ext = tm + 2 * CONV_HALO

    @pl.when(j == 0)
    def _():
        first = (i % tiles_per_seq) == 0
        last = (i % tiles_per_seq) == tiles_per_seq - 1
        lhs_scr[0:CONV_HALO, :] = jnp.where(first, jnp.zeros_like(hp_ref), hp_ref[...])
        lhs_scr[CONV_HALO:CONV_HALO + tm, :] = hc_ref[...]
        lhs_scr[CONV_HALO + tm:n_ext, :] = jnp.where(last, jnp.zeros_like(hn_ref), hn_ref[...])

    lhs = lhs_scr[...]

    def conv(w_ref, cw_ref, cb_ref):
        a = _dot(lhs, w_ref[...])
        cw = cw_ref[...]
        c = (pltpu.roll(a, 1, axis=0) * cw[0:1] + a * cw[1:2] + pltpu.roll(a, n_ext - 1, axis=0) * cw[2:3]
             + cb_ref[...])
        return c[CONV_HALO:CONV_HALO + tm, :]

    gate = conv(wg_ref, cwg_ref, cbg_ref)
    up = conv(wu_ref, cwu_ref, cbu_ref)
    o_ref[...] = (gate * jax.nn.sigmoid(gate) * up).astype(o_ref.dtype)


def mlp_up(h, w_up, conv_w, conv_b, *, tm, seq):
    r, d = h.shape
    f = w_up.shape[1] // 2
    fc = 512
    nf = f // fc
    per16 = tm // CONV_HALO
    n16 = r // CONV_HALO
    tiles_per_seq = seq // tm
    return pl.pallas_call(
        functools.partial(_mlp_up_kernel, tiles_per_seq=tiles_per_seq),
        out_shape=jax.ShapeDtypeStruct((r, f), BF16),
        grid=(r // tm, nf),
        in_specs=[
            pl.BlockSpec((CONV_HALO, d), lambda i, j: (jnp.maximum(i * per16 - 1, 0), 0)),
            pl.BlockSpec((tm, d), lambda i, j: (i, 0)),
            pl.BlockSpec((CONV_HALO, d), lambda i, j: (jnp.minimum((i + 1) * per16, n16 - 1), 0)),
            pl.BlockSpec((d, fc), lambda i, j: (0, j)),
            pl.BlockSpec((d, fc), lambda i, j: (0, nf + j)),
            pl.BlockSpec((3, fc), lambda i, j: (0, j)),
            pl.BlockSpec((3, fc), lambda i, j: (0, nf + j)),
            pl.BlockSpec((1, fc), lambda i, j: (0, j)),
            pl.BlockSpec((1, fc), lambda i, j: (0, nf + j)),
        ],
        out_specs=pl.BlockSpec((tm, fc), lambda i, j: (i, j)),
        scratch_shapes=[pltpu.VMEM((tm + 2 * CONV_HALO, d), BF16)],
        compiler_params=_cparams(("arbitrary", "arbitrary")),
        name="mlp_up_conv_gate",
    )(h, h, h, w_up, w_up, conv_w, conv_w, conv_b, conv_b)


def _mlp_down_kernel(g_ref, w_ref, x_ref, gf_ref, gfin_ref, o_ref, acc_scr, *, final_norm):
    k = pl.program_id(1)

    @pl.when(k == 0)
    def _():
        acc_scr[...] = jnp.zeros(acc_scr.shape, F32)

    acc_scr[...] += _dot(g_ref[...], w_ref[...])

    @pl.when(k == pl.num_programs(1) - 1)
    def _():
        x = x_ref[...] + gf_ref[...] * acc_scr[...]
        if final_norm:
            x = x * lax.rsqrt(jnp.mean(x * x, axis=-1, keepdims=True) + EPS) * gfin_ref[...]
        o_ref[...] = x


def mlp_down_residual(g, w_down, x, mod4, layer, mod_rows, g_final, *, tm, seq, final_norm):
    r, d = x.shape
    f = g.shape[1]
    kc = 512
    tiles_per_seq = seq // tm
    return pl.pallas_call(
        functools.partial(_mlp_down_kernel, final_norm=final_norm),
        out_shape=jax.ShapeDtypeStruct((r, d), F32),
        grid=(r // tm, f // kc),
        in_specs=[
            pl.BlockSpec((tm, kc), lambda i, k: (i, k)),
            pl.BlockSpec((kc, d), lambda i, k: (k, 0)),
            pl.BlockSpec((tm, d), lambda i, k: (i, 0)),
            pl.BlockSpec((None, None, 1, d), lambda i, k: (layer, mod_rows(i // tiles_per_seq), 0, 5)),
            pl.BlockSpec((1, d), lambda i, k: (0, 0)),
        ],
        out_specs=pl.BlockSpec((tm, d), lambda i, k: (i, 0)),
        scratch_shapes=[pltpu.VMEM((tm, d), F32)],
        compiler_params=_cparams(("arbitrary", "arbitrary")),
        name="mlp_down_residual",
    )(g, w_down, x, mod4, g_final)


def rope_tables(n_tokens):
    t = jnp.arange(n_tokens)
    n_freq = HEAD_DIM // 4
    inv = ROPE_BASE ** (-jnp.arange(n_freq, dtype=F32) / n_freq)
    ang_r = (t // GRID_W).astype(F32)[:, None] * inv
    ang_c = (t % GRID_W).astype(F32)[:, None] * inv
    cos = jnp.concatenate([jnp.cos(ang_r), jnp.cos(ang_r), jnp.cos(ang_c), jnp.cos(ang_c)], axis=-1)
    sin = jnp.concatenate([-jnp.sin(ang_r), jnp.sin(ang_r), -jnp.sin(ang_c), jnp.sin(ang_c)], axis=-1)
    return cos, sin


def kernel(x, c, ctx, c_ctx, w_mod, b_mod, g_mix, w_in, na_bias, w_pool, pool_scale, lambda_q1, lambda_k1,
           lambda_q2, lambda_k2, g_subln, w_out, g_ffn, w_up, conv_w, conv_b, w_down, g_final):
    batch, seq, d = x.shape
    ctx_len = ctx.shape[1]
    depth = w_mod.shape[0]
    rows = seq // GRID_W

    cond_t = jnp.zeros((d, 8), F32).at[:, :batch].set(c.T).at[:, batch].set(c_ctx)
    mod = ada_modulation(cond_t, w_mod, b_mod, batch + 1)
    mod4 = mod.reshape(depth, 8, 1, 6 * d)
    x_rows = lambda s: s
    ctx_rows = lambda s: batch

    cos_x, sin_x = rope_tables(seq)
    cos_c, sin_c = jnp.ones((ctx_len, HEAD_DIM), F32), jnp.zeros((ctx_len, HEAD_DIM), F32)

    w_in_b, w_out_b = w_in.astype(BF16), w_out.astype(BF16)
    w_up_b, w_down_b, w_pool_b = w_up.astype(BF16), w_down.astype(BF16), w_pool.astype(BF16)

    xs = x.reshape(batch * seq, d)
    cs = ctx.reshape(batch * ctx_len, d)
    tm_c = ctx_len
    for layer in range(depth):
        last = layer == depth - 1
        lam_init = 0.8 - 0.6 * math.exp(-0.3 * layer)
        lam_vecs = jnp.stack([lambda_q1[layer], lambda_k1[layer], lambda_q2[layer], lambda_k2[layer]])
        g_mix_l, g_ffn_l = g_mix[layer][None], g_ffn[layer][None]
        g_sub_l, ps_l = g_subln[layer][None], pool_scale[layer][None]
        bias_tabs = na_bias_tables(na_bias[layer], rows)

        u_x, ub_x = norm_inproj(xs, g_mix_l, mod4, layer, x_rows, w_in_b[layer], cos_x, sin_x, tm=1024, seq=seq)
        u_c, ub_c = norm_inproj(cs, g_mix_l, mod4, layer, ctx_rows, w_in_b[layer], cos_c, sin_c,
                                tm=tm_c, seq=ctx_len)

        oa_x = neighbourhood_attention(u_x, u_c, bias_tabs, batch=batch, seq=seq, ctx_len=ctx_len)
        ob_x = multiscale_pool(ub_x, w_pool_b[layer], ps_l, batch=batch, seq=seq, tm=512)
        oc_x = differential_attention(u_x, u_c, lam_vecs, g_sub_l, lam_init, batch=batch, seq=seq,
                                      ctx_len=ctx_len, tq=1024, tk=512)
        x_new, h_x = outproj_residual(oa_x, ob_x, oc_x, w_out_b[layer], xs, g_ffn_l, mod4, layer, x_rows,
                                      tm=512, seq=seq)
        g_x = mlp_up(h_x, w_up_b[layer], conv_w[layer], conv_b[layer][None], tm=1024, seq=seq)
        xs = mlp_down_residual(g_x, w_down_b[layer], x_new, mod4, layer, x_rows, g_final[None], tm=512, seq=seq,
                               final_norm=last)

        if not last:
            oa_c = context_attention(u_c, batch=batch, ctx_len=ctx_len)
            ob_c = multiscale_pool(ub_c, w_pool_b[layer], ps_l, batch=batch, seq=ctx_len, tm=tm_c)
            oc_c = differential_attention(u_c, None, lam_vecs, g_sub_l, lam_init, batch=batch, seq=ctx_len,
                                          ctx_len=ctx_len, tq=tm_c, tk=tm_c)
            c_new, h_c = outproj_residual(oa_c, ob_c, oc_c, w_out_b[layer], cs, g_ffn_l, mod4, layer, ctx_rows,
                                          tm=tm_c, seq=ctx_len)
            g_c = mlp_up(h_c, w_up_b[layer], conv_w[layer], conv_b[layer][None], tm=tm_c, seq=ctx_len)
            cs = mlp_down_residual(g_c, w_down_b[layer], c_new, mod4, layer, ctx_rows, g_final[None], tm=tm_c,
                                   seq=ctx_len, final_norm=False)
    return xs.reshape(batch, seq, d)
```

```python
import functools
import math

import jax
import jax.numpy as jnp
import numpy as np
from jax import lax
from jax.experimental import pallas as pl
from jax.experimental.pallas import tpu as pltpu

F32 = jnp.float32
BF16 = jnp.bfloat16

GRID_W = 64
HEAD_DIM = 128
NA_HEADS = 4
NA_WIN_H = 8
NA_WIN_W = 16
NA_ROWS_PER_BLOCK = 4
POOL_WINDOWS = (2, 4, 8, 16)
POOL_GROUP = 128
POOL_HALO = 8
DA_HEADS = 4
DA_V_DIM = 256
CONV_HALO = 16
ROPE_BASE = 10000.0
EPS = 1e-6
NEG = -1e30
DA_Q_SCALE = HEAD_DIM ** -0.5 * math.log2(math.e)
DA_Q_CHUNK = 256
VMEM_LIMIT = 56 * 1024 * 1024


def _cparams(sem):
    return pltpu.CompilerParams(dimension_semantics=sem, vmem_limit_bytes=VMEM_LIMIT)


def _dot(a, b):
    return jnp.dot(a, b, preferred_element_type=F32)


def _dot_nt(a, b):
    return lax.dot_general(a, b, (((1,), (1,)), ((), ())), preferred_element_type=F32)


def _dot_tn(a, b):
    return lax.dot_general(a, b, (((0,), (0,)), ((), ())), preferred_element_type=F32)


def _mod_kernel(ct_ref, w_ref, b_ref, o_ref, *, n_cond):
    ct = ct_ref[...]
    a = ct * jax.nn.sigmoid(ct)
    w = w_ref[...]
    rows = [jnp.sum(a[:, r:r + 1] * w, axis=0, keepdims=True) for r in range(n_cond)]
    rows.append(jnp.zeros((8 - n_cond, w.shape[1]), F32))
    o_ref[...] = jnp.concatenate(rows, axis=0) + b_ref[...]


def ada_modulation(cond_t, w_mod, b_mod, n_cond):
    depth, d, n = w_mod.shape
    tn = 1024
    return pl.pallas_call(
        functools.partial(_mod_kernel, n_cond=n_cond),
        out_shape=jax.ShapeDtypeStruct((depth, 8, n), F32),
        grid=(depth, n // tn),
        in_specs=[
            pl.BlockSpec((d, 8), lambda l, j: (0, 0)),
            pl.BlockSpec((None, d, tn), lambda l, j: (l, 0, j)),
            pl.BlockSpec((None, 1, tn), lambda l, j: (l, 0, j)),
        ],
        out_specs=pl.BlockSpec((None, 8, tn), lambda l, j: (l, 0, j)),
        compiler_params=_cparams(("arbitrary", "arbitrary")),
        name="ada_modulation",
    )(cond_t, w_mod, b_mod.reshape(depth, 1, n))


def _swap32(x):
    lane = lax.broadcasted_iota(jnp.int32, x.shape, 1)
    return jnp.where((lane % 64) < 32, pltpu.roll(x, 96, axis=1), pltpu.roll(x, 32, axis=1))


def _inproj_kernel(x_ref, g_ref, sh_ref, sc_ref, w_ref, cos_ref, sin_ref, u_ref, ub_ref, h_scr,
                   *, pool_block, rope_blocks):
    j = pl.program_id(1)

    @pl.when(j == 0)
    def _():
        x = x_ref[...]
        y = x * lax.rsqrt(jnp.mean(x * x, axis=-1, keepdims=True) + EPS)
        h = y * (g_ref[...] * (1.0 + sc_ref[...])) + sh_ref[...]
        h_scr[...] = h.astype(BF16)

    acc = _dot(h_scr[...], w_ref[...])

    is_rope = (j >= rope_blocks[0]) & (j < rope_blocks[1])

    @pl.when(is_rope)
    def _():
        qk_scale = jnp.where(j < rope_blocks[2], DA_Q_SCALE, 1.0)
        cos = cos_ref[...] * qk_scale
        sin = sin_ref[...] * qk_scale
        for c in range(acc.shape[1] // HEAD_DIM):
            xc = acc[:, c * HEAD_DIM:(c + 1) * HEAD_DIM]
            u_ref[:, c * HEAD_DIM:(c + 1) * HEAD_DIM] = (xc * cos + _swap32(xc) * sin).astype(BF16)

    @pl.when(jnp.logical_not(is_rope))
    def _():
        u_ref[...] = acc.astype(BF16)

    @pl.when(j == pool_block)
    def _():
        ub_ref[...] = acc


def norm_inproj(x, g, mod4, layer, mod_rows, w_in, cos, sin, *, tm, seq):
    r, d = x.shape
    d_in = w_in.shape[1]
    tn = 512
    tiles_per_seq = seq // tm
    pool_block = 1536 // tn
    rope_blocks = (2048 // tn, 4096 // tn, 3072 // tn)

    def mod_spec(chunk):
        return pl.BlockSpec((None, None, 1, d),
                            lambda i, j: (layer, mod_rows(i // tiles_per_seq), 0, chunk))

    return pl.pallas_call(
        functools.partial(_inproj_kernel, pool_block=pool_block, rope_blocks=rope_blocks),
        out_shape=(jax.ShapeDtypeStruct((r, d_in), BF16), jax.ShapeDtypeStruct((r, 512), F32)),
        grid=(r // tm, d_in // tn),
        in_specs=[
            pl.BlockSpec((tm, d), lambda i, j: (i, 0)),
            pl.BlockSpec((1, d), lambda i, j: (0, 0)),
            mod_spec(0),
            mod_spec(1),
            pl.BlockSpec((d, tn), lambda i, j: (0, j)),
            pl.BlockSpec((tm, HEAD_DIM), lambda i, j: (i % tiles_per_seq, 0)),
            pl.BlockSpec((tm, HEAD_DIM), lambda i, j: (i % tiles_per_seq, 0)),
        ],
        out_specs=(pl.BlockSpec((tm, tn), lambda i, j: (i, j)),
                   pl.BlockSpec((tm, 512), lambda i, j: (i, 0))),
        scratch_shapes=[pltpu.VMEM((tm, d), BF16)],
        compiler_params=_cparams(("arbitrary", "arbitrary")),
        name="norm_inproj",
    )(x, g, mod4, mod4, w_in, cos, sin)


def _softmax_pv(score_blocks, value_blocks):
    m = functools.reduce(jnp.maximum, [jnp.max(s, axis=-1, keepdims=True) for s in score_blocks])
    ps = [jnp.exp(s - m) for s in score_blocks]
    l = functools.reduce(jnp.add, [jnp.sum(p, axis=-1, keepdims=True) for p in ps])
    o = functools.reduce(jnp.add, [_dot(p.astype(BF16), v) for p, v in zip(ps, value_blocks)])
    return o / l


def _na_kernel(q_ref, kp_ref, kc_ref, kn_ref, vp_ref, vc_ref, vn_ref, kx_ref, vx_ref, bias_ref, o_ref):
    scale = HEAD_DIM ** -0.5
    nq = q_ref.shape[0]
    for h in range(NA_HEADS):
        sl = slice(h * HEAD_DIM, (h + 1) * HEAD_DIM)
        q = q_ref[:, sl]
        scores = []
        for n, k_ref in enumerate((kp_ref, kc_ref, kn_ref)):
            scores.append(_dot_nt(q, k_ref[:, sl]) * scale + bias_ref[h, :, n * nq:(n + 1) * nq])
        scores.append(_dot_nt(q, kx_ref[:, sl]) * scale)
        values = [vp_ref[:, sl], vc_ref[:, sl], vn_ref[:, sl], vx_ref[:, sl]]
        o_ref[:, sl] = _softmax_pv(scores, values).astype(o_ref.dtype)


def na_bias_tables(rel_bias, rows):
    rb = NA_ROWS_PER_BLOCK
    nblk = rows // rb
    qr = np.arange(rb)[:, None, None, None]
    qc = np.arange(GRID_W)[None, :, None, None]
    kk = np.arange(3 * rb)[None, None, :, None]
    kc = np.arange(GRID_W)[None, None, None, :]
    cs = np.clip(qc - NA_WIN_W // 2, 0, GRID_W - NA_WIN_W)
    col_ok = (kc >= cs) & (kc < cs + NA_WIN_W)
    ok = []
    for j in (0, 1, nblk - 1):
        r = rb * j + qr
        kr = rb * (j - 1) + kk
        rs = np.clip(r - NA_WIN_H // 2, 0, rows - NA_WIN_H)
        ok.append(np.broadcast_to((kr >= rs) & (kr < rs + NA_WIN_H) & col_ok, (rb, GRID_W, 3 * rb, GRID_W)))
    ok = np.stack(ok)[:, None]
    row_off = (kk - qr - rb + NA_WIN_H - 1)[:, 0, :, 0]
    col_off = (kc - qc + NA_WIN_W - 1)[0, :, 0, :]
    rsel = (row_off[None] == np.arange(2 * NA_WIN_H - 1)[:, None, None]).astype(np.float32)
    csel = (col_off[None] == np.arange(2 * NA_WIN_W - 1)[:, None, None]).astype(np.float32)
    rel = jnp.einsum("hab,aqk,bcd->hqckd", rel_bias.astype(F32), rsel, csel, precision=lax.Precision.HIGHEST)
    tabs = jnp.where(ok, rel[None], NEG)
    return tabs.reshape(3, rel_bias.shape[0], rb * GRID_W, 3 * rb * GRID_W)


def neighbourhood_attention(u_x, u_c, bias_tabs, *, batch, seq, ctx_len):
    rows = seq // GRID_W
    assert rows % NA_ROWS_PER_BLOCK == 0 and rows >= 3 * NA_ROWS_PER_BLOCK and rows >= 2 * NA_WIN_H
    nq = NA_ROWS_PER_BLOCK * GRID_W
    nblk = seq // nq
    dn = NA_HEADS * HEAD_DIM

    def blk(col, off):
        def index(b, j):
            return (b * nblk + jnp.clip(j + off, 0, nblk - 1), col)
        return pl.BlockSpec((nq, dn), index)

    def variant(b, j):
        return (jnp.where(j == 0, 0, jnp.where(j == nblk - 1, 2, 1)), 0, 0, 0)

    return pl.pallas_call(
        _na_kernel,
        out_shape=jax.ShapeDtypeStruct((batch * seq, dn), BF16),
        grid=(batch, nblk),
        in_specs=[
            blk(0, 0),
            blk(1, -1), blk(1, 0), blk(1, 1),
            blk(2, -1), blk(2, 0), blk(2, 1),
            pl.BlockSpec((ctx_len, dn), lambda b, j: (b, 1)),
            pl.BlockSpec((ctx_len, dn), lambda b, j: (b, 2)),
            pl.BlockSpec((None, NA_HEADS, nq, 3 * nq), variant),
        ],
        out_specs=pl.BlockSpec((nq, dn), lambda b, j: (b * nblk + j, 0)),
        compiler_params=_cparams(("arbitrary", "arbitrary")),
        name="neighbourhood_attention",
    )(u_x, u_x, u_x, u_x, u_x, u_x, u_x, u_c, u_c, bias_tabs)


def _ctx_attn_kernel(q_ref, k_ref, v_ref, o_ref):
    scale = HEAD_DIM ** -0.5
    for h in range(NA_HEADS):
        sl = slice(h * HEAD_DIM, (h + 1) * HEAD_DIM)
        s = _dot_nt(q_ref[:, sl], k_ref[:, sl]) * scale
        o_ref[:, sl] = _softmax_pv([s], [v_ref[:, sl]]).astype(o_ref.dtype)


def context_attention(u_c, *, batch, ctx_len):
    dn = NA_HEADS * HEAD_DIM
    return pl.pallas_call(
        _ctx_attn_kernel,
        out_shape=jax.ShapeDtypeStruct((batch * ctx_len, dn), BF16),
        grid=(batch,),
        in_specs=[pl.BlockSpec((ctx_len, dn), lambda b, c=c: (b, c)) for c in range(3)],
        out_specs=pl.BlockSpec((ctx_len, dn), lambda b: (b, 0)),
        compiler_params=_cparams(("arbitrary",)),
        name="context_attention",
    )(u_c, u_c, u_c)


def _pool_kernel(up_ref, uc_ref, un_ref, w_ref, ps_ref, o_ref, ext_scr, *, seq):
    i = pl.program_id(1)
    n_i = pl.num_programs(1)
    tm = uc_ref.shape[0]
    n_ext = tm + 2 * POOL_HALO
    ext_scr[0:POOL_HALO, :] = jnp.where(i == 0, 0.0, up_ref[...])
    ext_scr[POOL_HALO:POOL_HALO + tm, :] = uc_ref[...]
    ext_scr[POOL_HALO + tm:n_ext, :] = jnp.where(i == n_i - 1, 0.0, un_ref[...])
    t = i * tm + lax.broadcasted_iota(jnp.int32, (tm, 1), 0)
    for g, w in enumerate(POOL_WINDOWS):
        sl = slice(g * POOL_GROUP, (g + 1) * POOL_GROUP)
        s = ext_scr[:, sl]
        s = s + pltpu.roll(s, 1, axis=0)
        half = 1
        while 2 * half < w:
            s = pltpu.roll(s, half, axis=0) + pltpu.roll(s, n_ext - half, axis=0)
            half *= 2
        count = jnp.minimum(t + w // 2, seq) - jnp.maximum(t - w // 2, 0)
        u = uc_ref[:, sl]
        pooled = s[POOL_HALO:POOL_HALO + tm, :] / count.astype(F32) - u
        y = _dot(pooled.astype(BF16), w_ref[g])
        o_ref[:, sl] = (y * ps_ref[:, sl]).astype(o_ref.dtype)


def multiscale_pool(ub, w_pool, pool_scale, *, batch, seq, tm):
    dp = ub.shape[1]
    n_i = seq // tm
    per8 = tm // POOL_HALO
    n8 = batch * seq // POOL_HALO
    return pl.pallas_call(
        functools.partial(_pool_kernel, seq=seq),
        out_shape=jax.ShapeDtypeStruct((batch * seq, dp), BF16),
        grid=(batch, n_i),
        in_specs=[
            pl.BlockSpec((POOL_HALO, dp), lambda b, i: (jnp.maximum((b * n_i + i) * per8 - 1, 0), 0)),
            pl.BlockSpec((tm, dp), lambda b, i: (b * n_i + i, 0)),
            pl.BlockSpec((POOL_HALO, dp), lambda b, i: (jnp.minimum((b * n_i + i + 1) * per8, n8 - 1), 0)),
            pl.BlockSpec(w_pool.shape, lambda b, i: (0, 0, 0)),
            pl.BlockSpec((1, dp), lambda b, i: (0, 0)),
        ],
        out_specs=pl.BlockSpec((tm, dp), lambda b, i: (b * n_i + i, 0)),
        scratch_shapes=[pltpu.VMEM((tm + 2 * POOL_HALO, dp), F32)],
        compiler_params=_cparams(("arbitrary", "arbitrary")),
        name="multiscale_pool",
    )(ub, ub, ub, w_pool, pool_scale)


def _da_kernel(lam_ref, q_ref, k_ref, v_ref, *rest, has_ctx, lam_init):
    if has_ctx:
        kx_ref, vx_ref, g_ref, o_ref, m_scr, l_scr, acc_scr = rest
    else:
        g_ref, o_ref, m_scr, l_scr, acc_scr = rest
    kj = pl.program_id(3)

    @pl.when(kj == 0)
    def _():
        m_scr[...] = jnp.full(m_scr.shape, NEG, F32)
        l_scr[...] = jnp.zeros(l_scr.shape, F32)
        acc_scr[...] = jnp.zeros(acc_scr.shape, F32)

    def update(k_blk_ref, v_blk_ref):
        v = v_blk_ref[...]
        tq = q_ref.shape[0]
        chains = [(c, slice(i * DA_Q_CHUNK, (i + 1) * DA_Q_CHUNK))
                  for i in range(tq // DA_Q_CHUNK) for c in range(2)]

        def scores(c, qs):
            sl = slice(c * HEAD_DIM, (c + 1) * HEAD_DIM)
            s = _dot_nt(k_blk_ref[:, sl], q_ref[qs, sl])
            m_old = m_scr[c, :, qs]
            m_new = jnp.maximum(m_old, jnp.max(s, axis=0, keepdims=True))
            return c, qs, s, m_old, m_new

        def probs(c, qs, s, m_old, m_new):
            alpha = jnp.exp2(m_old - m_new)
            p = jnp.exp2(s - m_new)
            l_scr[c, :, qs] = alpha * l_scr[c, :, qs] + jnp.sum(p, axis=0, keepdims=True)
            m_scr[c, :, qs] = m_new
            return c, qs, alpha, p.astype(BF16)

        def accumulate(c, qs, alpha, p):
            acc_scr[c, :, qs] = alpha * acc_scr[c, :, qs] + _dot_tn(v, p)

        after_scores = after_probs = None
        for n in range(len(chains) + 2):
            new_scores = scores(*chains[n]) if n < len(chains) else None
            new_probs = probs(*after_scores) if after_scores is not None else None
            if after_probs is not None:
                accumulate(*after_probs)
            after_scores, after_probs = new_scores, new_probs

    if has_ctx:
        @pl.when(kj == 0)
        def _():
            update(kx_ref, vx_ref)

    update(k_ref, v_ref)

    @pl.when(kj == pl.num_programs(3) - 1)
    def _():
        lv = lam_ref[...]
        lam = (jnp.exp(jnp.sum(lv[0:1] * lv[1:2], axis=-1, keepdims=True))
               - jnp.exp(jnp.sum(lv[2:3] * lv[3:4], axis=-1, keepdims=True)) + lam_init)
        o = acc_scr[0] / l_scr[0] - lam * (acc_scr[1] / l_scr[1])
        y = o * lax.rsqrt(jnp.mean(o * o, axis=0, keepdims=True) + EPS)
        o_ref[...] = (y.T * g_ref[...] * (1.0 - lam_init)).astype(o_ref.dtype)


def differential_attention(u_q, u_ctx, lam_vecs, g_subln, lam_init, *, batch, seq, ctx_len, tq, tk):
    has_ctx = u_ctx is not None
    nq, nk = seq // tq, seq // tk
    qcol, kcol, vcol = 2048 // DA_V_DIM, 3072 // DA_V_DIM, 4096 // DA_V_DIM
    in_specs = [
        pl.BlockSpec((4, HEAD_DIM), lambda b, h, i, j: (0, 0)),
        pl.BlockSpec((tq, DA_V_DIM), lambda b, h, i, j: (b * nq + i, qcol + h)),
        pl.BlockSpec((tk, DA_V_DIM), lambda b, h, i, j: (b * nk + j, kcol + h)),
        pl.BlockSpec((tk, DA_V_DIM), lambda b, h, i, j: (b * nk + j, vcol + h)),
    ]
    args = [lam_vecs, u_q, u_q, u_q]
    if has_ctx:
        in_specs += [
            pl.BlockSpec((ctx_len, DA_V_DIM), lambda b, h, i, j: (b, kcol + h)),
            pl.BlockSpec((ctx_len, DA_V_DIM), lambda b, h, i, j: (b, vcol + h)),
        ]
        args += [u_ctx, u_ctx]
    in_specs.append(pl.BlockSpec((1, DA_V_DIM), lambda b, h, i, j: (0, 0)))
    args.append(g_subln)
    return pl.pallas_call(
        functools.partial(_da_kernel, has_ctx=has_ctx, lam_init=lam_init),
        out_shape=jax.ShapeDtypeStruct((batch * seq, DA_HEADS * DA_V_DIM), BF16),
        grid=(batch, DA_HEADS, nq, nk),
        in_specs=in_specs,
        out_specs=pl.BlockSpec((tq, DA_V_DIM), lambda b, h, i, j: (b * nq + i, h)),
        scratch_shapes=[pltpu.VMEM((2, 1, tq), F32), pltpu.VMEM((2, 1, tq), F32),
                        pltpu.VMEM((2, DA_V_DIM, tq), F32)],
        compiler_params=_cparams(("arbitrary",) * 4),
        name="differential_attention",
    )(*args)


def _outproj_kernel(oa_ref, ob_ref, oc_ref, w_ref, x_ref, ga_ref, g_ref, sh_ref, sc_ref, xo_ref, h_ref):
    na, nb = oa_ref.shape[1], ob_ref.shape[1]
    y = (_dot(oa_ref[...], w_ref[0:na, :]) + _dot(ob_ref[...], w_ref[na:na + nb, :])
         + _dot(oc_ref[...], w_ref[na + nb:, :]))
    x = x_ref[...] + ga_ref[...] * y
    xo_ref[...] = x
    z = x * lax.rsqrt(jnp.mean(x * x, axis=-1, keepdims=True) + EPS)
    h_ref[...] = (z * (g_ref[...] * (1.0 + sc_ref[...])) + sh_ref[...]).astype(BF16)


def outproj_residual(oa, ob, oc, w_out, x, g_ffn, mod4, layer, mod_rows, *, tm, seq):
    r, d = x.shape
    tiles_per_seq = seq // tm

    def mod_spec(chunk):
        return pl.BlockSpec((None, None, 1, d),
                            lambda i: (layer, mod_rows(i // tiles_per_seq), 0, chunk))

    def row_spec(a):
        return pl.BlockSpec((tm, a.shape[1]), lambda i: (i, 0))

    return pl.pallas_call(
        _outproj_kernel,
        out_shape=(jax.ShapeDtypeStruct((r, d), F32), jax.ShapeDtypeStruct((r, d), BF16)),
        grid=(r // tm,),
        in_specs=[row_spec(oa), row_spec(ob), row_spec(oc),
                  pl.BlockSpec(w_out.shape, lambda i: (0, 0)),
                  row_spec(x), mod_spec(2),
                  pl.BlockSpec((1, d), lambda i: (0, 0)), mod_spec(3), mod_spec(4)],
        out_specs=(pl.BlockSpec((tm, d), lambda i: (i, 0)), pl.BlockSpec((tm, d), lambda i: (i, 0))),
        compiler_params=_cparams(("arbitrary",)),
        name="outproj_residual",
    )(oa, ob, oc, w_out, x, mod4, g_ffn, mod4, mod4)


def _mlp_up_kernel(hp_ref, hc_ref, hn_ref, wg_ref, wu_ref, cwg_ref, cwu_ref, cbg_ref, cbu_ref, o_ref, lhs_scr,
                   *, tiles_per_seq):
    i = pl.program_id(0)
    j = pl.program_id(1)
    tm = hc_ref.shape[0]
    n_ext = tm + 2 * CONV_HALO

    @pl.when(j == 0)
    def _():
        first = (i % tiles_per_seq) == 0
        last = (i % tiles_per_seq) == tiles_per_seq - 1
        lhs_scr[0:CONV_HALO, :] = jnp.where(first, jnp.zeros_like(hp_ref), hp_ref[...])
        lhs_scr[CONV_HALO:CONV_HALO + tm, :] = hc_ref[...]
        lhs_scr[CONV_HALO + tm:n_ext, :] = jnp.where(last, jnp.zeros_like(hn_ref), hn_ref[...])

    lhs = lhs_scr[...]

    def conv(w_ref, cw_ref, cb_ref):
        a = _dot(lhs, w_ref[...])
        cw = cw_ref[...]
        c = (pltpu.roll(a, 1, axis=0) * cw[0:1] + a * cw[1:2] + pltpu.roll(a, n_ext - 1, axis=0) * cw[2:3]
             + cb_ref[...])
        return c[CONV_HALO:CONV_HALO + tm, :]

    gate = conv(wg_ref, cwg_ref, cbg_ref)
    up = conv(wu_ref, cwu_ref, cbu_ref)
    o_ref[...] = (gate * jax.nn.sigmoid(gate) * up).astype(o_ref.dtype)


def mlp_up(h, w_up, conv_w, conv_b, *, tm, seq):
    r, d = h.shape
    f = w_up.shape[1] // 2
    fc = 512
    nf = f // fc
    per16 = tm // CONV_HALO
    n16 = r // CONV_HALO
    tiles_per_seq = seq // tm
    return pl.pallas_call(
        functools.partial(_mlp_up_kernel, tiles_per_seq=tiles_per_seq),
        out_shape=jax.ShapeDtypeStruct((r, f), BF16),
        grid=(r // tm, nf),
        in_specs=[
            pl.BlockSpec((CONV_HALO, d), lambda i, j: (jnp.maximum(i * per16 - 1, 0), 0)),
            pl.BlockSpec((tm, d), lambda i, j: (i, 0)),
            pl.BlockSpec((CONV_HALO, d), lambda i, j: (jnp.minimum((i + 1) * per16, n16 - 1), 0)),
            pl.BlockSpec((d, fc), lambda i, j: (0, j)),
            pl.BlockSpec((d, fc), lambda i, j: (0, nf + j)),
            pl.BlockSpec((3, fc), lambda i, j: (0, j)),
            pl.BlockSpec((3, fc), lambda i, j: (0, nf + j)),
            pl.BlockSpec((1, fc), lambda i, j: (0, j)),
            pl.BlockSpec((1, fc), lambda i, j: (0, nf + j)),
        ],
        out_specs=pl.BlockSpec((tm, fc), lambda i, j: (i, j)),
        scratch_shapes=[pltpu.VMEM((tm + 2 * CONV_HALO, d), BF16)],
        compiler_params=_cparams(("arbitrary", "arbitrary")),
        name="mlp_up_conv_gate",
    )(h, h, h, w_up, w_up, conv_w, conv_w, conv_b, conv_b)


def _mlp_down_kernel(g_ref, w_ref, x_ref, gf_ref, gfin_ref, o_ref, acc_scr, *, final_norm):
    k = pl.program_id(1)

    @pl.when(k == 0)
    def _():
        acc_scr[...] = jnp.zeros(acc_scr.shape, F32)

    acc_scr[...] += _dot(g_ref[...], w_ref[...])

    @pl.when(k == pl.num_programs(1) - 1)
    def _():
        x = x_ref[...] + gf_ref[...] * acc_scr[...]
        if final_norm:
            x = x * lax.rsqrt(jnp.mean(x * x, axis=-1, keepdims=True) + EPS) * gfin_ref[...]
        o_ref[...] = x


def mlp_down_residual(g, w_down, x, mod4, layer, mod_rows, g_final, *, tm, seq, final_norm):
    r, d = x.shape
    f = g.shape[1]
    kc = 512
    tiles_per_seq = seq // tm
    return pl.pallas_call(
        functools.partial(_mlp_down_kernel, final_norm=final_norm),
        out_shape=jax.ShapeDtypeStruct((r, d), F32),
        grid=(r // tm, f // kc),
        in_specs=[
            pl.BlockSpec((tm, kc), lambda i, k: (i, k)),
            pl.BlockSpec((kc, d), lambda i, k: (k, 0)),
            pl.BlockSpec((tm, d), lambda i, k: (i, 0)),
            pl.BlockSpec((None, None, 1, d), lambda i, k: (layer, mod_rows(i // tiles_per_seq), 0, 5)),
            pl.BlockSpec((1, d), lambda i, k: (0, 0)),
        ],
        out_specs=pl.BlockSpec((tm, d), lambda i, k: (i, 0)),
        scratch_shapes=[pltpu.VMEM((tm, d), F32)],
        compiler_params=_cparams(("arbitrary", "arbitrary")),
        name="mlp_down_residual",
    )(g, w_down, x, mod4, g_final)


def rope_tables(n_tokens):
    t = jnp.arange(n_tokens)
    n_freq = HEAD_DIM // 4
    inv = ROPE_BASE ** (-jnp.arange(n_freq, dtype=F32) / n_freq)
    ang_r = (t // GRID_W).astype(F32)[:, None] * inv
    ang_c = (t % GRID_W).astype(F32)[:, None] * inv
    cos = jnp.concatenate([jnp.cos(ang_r), jnp.cos(ang_r), jnp.cos(ang_c), jnp.cos(ang_c)], axis=-1)
    sin = jnp.concatenate([-jnp.sin(ang_r), jnp.sin(ang_r), -jnp.sin(ang_c), jnp.sin(ang_c)], axis=-1)
    return cos, sin


def kernel(x, c, ctx, c_ctx, w_mod, b_mod, g_mix, w_in, na_bias, w_pool, pool_scale, lambda_q1, lambda_k1,
           lambda_q2, lambda_k2, g_subln, w_out, g_ffn, w_up, conv_w, conv_b, w_down, g_final):
    batch, seq, d = x.shape
    ctx_len = ctx.shape[1]
    depth = w_mod.shape[0]
    rows = seq // GRID_W

    cond_t = jnp.zeros((d, 8), F32).at[:, :batch].set(c.T).at[:, batch].set(c_ctx)
    mod = ada_modulation(cond_t, w_mod, b_mod, batch + 1)
    mod4 = mod.reshape(depth, 8, 1, 6 * d)
    x_rows = lambda s: s
    ctx_rows = lambda s: batch

    cos_x, sin_x = rope_tables(seq)
    cos_c, sin_c = jnp.ones((ctx_len, HEAD_DIM), F32), jnp.zeros((ctx_len, HEAD_DIM), F32)

    w_in_b, w_out_b = w_in.astype(BF16), w_out.astype(BF16)
    w_up_b, w_down_b, w_pool_b = w_up.astype(BF16), w_down.astype(BF16), w_pool.astype(BF16)

    xs = x.reshape(batch * seq, d)
    cs = ctx.reshape(batch * ctx_len, d)
    tm_c = ctx_len
    for layer in range(depth):
        last = layer == depth - 1
        lam_init = 0.8 - 0.6 * math.exp(-0.3 * layer)
        lam_vecs = jnp.stack([lambda_q1[layer], lambda_k1[layer], lambda_q2[layer], lambda_k2[layer]])
        g_mix_l, g_ffn_l = g_mix[layer][None], g_ffn[layer][None]
        g_sub_l, ps_l = g_subln[layer][None], pool_scale[layer][None]
        bias_tabs = na_bias_tables(na_bias[layer], rows)

        u_x, ub_x = norm_inproj(xs, g_mix_l, mod4, layer, x_rows, w_in_b[layer], cos_x, sin_x, tm=1024, seq=seq)
        u_c, ub_c = norm_inproj(cs, g_mix_l, mod4, layer, ctx_rows, w_in_b[layer], cos_c, sin_c,
                                tm=tm_c, seq=ctx_len)

        oa_x = neighbourhood_attention(u_x, u_c, bias_tabs, batch=batch, seq=seq, ctx_len=ctx_len)
        ob_x = multiscale_pool(ub_x, w_pool_b[layer], ps_l, batch=batch, seq=seq, tm=512)
        oc_x = differential_attention(u_x, u_c, lam_vecs, g_sub_l, lam_init, batch=batch, seq=seq,
                                      ctx_len=ctx_len, tq=2048, tk=512)
        x_new, h_x = outproj_residual(oa_x, ob_x, oc_x, w_out_b[layer], xs, g_ffn_l, mod4, layer, x_rows,
                                      tm=512, seq=seq)
        g_x = mlp_up(h_x, w_up_b[layer], conv_w[layer], conv_b[layer][None], tm=1024, seq=seq)
        xs = mlp_down_residual(g_x, w_down_b[layer], x_new, mod4, layer, x_rows, g_final[None], tm=512, seq=seq,
                               final_norm=last)

        if not last:
            oa_c = context_attention(u_c, batch=batch, ctx_len=ctx_len)
            ob_c = multiscale_pool(ub_c, w_pool_b[layer], ps_l, batch=batch, seq=ctx_len, tm=tm_c)
            oc_c = differential_attention(u_c, None, lam_vecs, g_sub_l, lam_init, batch=batch, seq=ctx_len,
                                          ctx_len=ctx_len, tq=tm_c, tk=tm_c)
            c_new, h_c = outproj_residual(oa_c, ob_c, oc_c, w_out_b[layer], cs, g_ffn_l, mod4, layer, ctx_rows,
                                          tm=tm_c, seq=ctx_len)
            g_c = mlp_up(h_c, w_up_b[layer], conv_w[layer], conv_b[layer][None], tm=tm_c, seq=ctx_len)
            cs = mlp_down_residual(g_c, w_down_b[layer], c_new, mod4, layer, ctx_rows, g_final[None], tm=tm_c,
                                   seq=ctx_len, final_norm=False)
    return xs.reshape(batch, seq, d)
```
